```python
import math
import jax, jax.numpy as jnp
from jax import lax
import numpy as np

D_MODEL = 1024
BATCH = 8
SEQ = 4096
DEPTH = 2

CHUNK = 64
ATT_HEADS = 8
ATT_HEAD_DIM = D_MODEL // 16
ATT_WIDTH = ATT_HEADS * ATT_HEAD_DIM
LEFT_CHUNKS = 8
BAND = (LEFT_CHUNKS + 1) * CHUNK
MAX_REL = 128
RET_HEADS = 8
RET_QK_DIM = D_MODEL // 16
RET_V_DIM = D_MODEL // 8
RET_QK_WIDTH = RET_HEADS * RET_QK_DIM
RET_V_WIDTH = RET_HEADS * RET_V_DIM
ROPE_BASE = 10000.0
SSM_WIDTH = D_MODEL // 2
SSM_GROUP = 16
SSM_GROUPS = SSM_WIDTH // SSM_GROUP
SSM_STATE = 64
DT_MIN = 1e-3
DT_MAX = 1e-1
D_FF = 2816
CONV_WIDTH = 3
N_BRANCH = 3
IN_SIZES = (ATT_WIDTH, ATT_WIDTH, ATT_WIDTH, RET_QK_WIDTH, RET_QK_WIDTH, RET_V_WIDTH, RET_V_WIDTH,
            SSM_WIDTH, N_BRANCH * D_MODEL)
IN_OFFSETS = tuple(sum(IN_SIZES[:i]) for i in range(len(IN_SIZES) + 1))
IN_WIDTH = IN_OFFSETS[-1]
DEEPNORM_ALPHA = (2.0 * DEPTH) ** 0.25
DEEPNORM_BETA = (8.0 * DEPTH) ** -0.25
LN_EPS = 1e-5

kernel_name = "hybrid_chunk_causal_gated_merge_encoder"


def layer_norm(x, g, b):
    xf = x.astype(jnp.float32)
    mu = jnp.mean(xf, -1, keepdims=True)
    var = jnp.mean(jnp.square(xf - mu), -1, keepdims=True)
    y = (xf - mu) * lax.rsqrt(var + LN_EPS) * g.astype(jnp.float32) + b.astype(jnp.float32)
    return y.astype(x.dtype)


def in_cols(w, i):
    return w[:, IN_OFFSETS[i]:IN_OFFSETS[i + 1]]


def rotary(x):
    S, d = x.shape[1], x.shape[-1]
    inv = ROPE_BASE ** (-jnp.arange(0, d, 2, dtype=jnp.float32) / d)
    ang = jnp.arange(S, dtype=jnp.float32)[:, None] * inv[None, :]
    cos = jnp.cos(ang)[None, :, None, :]
    sin = jnp.sin(ang)[None, :, None, :]
    xf = x.astype(jnp.float32)
    x1, x2 = xf[..., : d // 2], xf[..., d // 2:]
    return jnp.concatenate([x1 * cos - x2 * sin, x1 * sin + x2 * cos], axis=-1)


def band_attention(q, k, v, rel_bias):
    B, S, H, dh = q.shape
    n_chunks = S // CHUNK
    pad = ((0, 0), (LEFT_CHUNKS * CHUNK, 0), (0, 0), (0, 0))
    kp = jnp.pad(k, pad).reshape(B, n_chunks + LEFT_CHUNKS, CHUNK, H, dh)
    vp = jnp.pad(v, pad).reshape(B, n_chunks + LEFT_CHUNKS, CHUNK, H, dh)
    k_band = jnp.concatenate([kp[:, j:j + n_chunks] for j in range(LEFT_CHUNKS + 1)], axis=2)
    v_band = jnp.concatenate([vp[:, j:j + n_chunks] for j in range(LEFT_CHUNKS + 1)], axis=2)
    qc = q.reshape(B, n_chunks, CHUNK, H, dh)
    s = jnp.einsum('bcqhd,bckhd->bhcqk', qc, k_band).astype(jnp.float32) * (dh ** -0.5)
    q_pos = jnp.arange(CHUNK) + LEFT_CHUNKS * CHUNK
    k_pos = jnp.arange(BAND)
    rel_idx = jnp.clip(q_pos[:, None] - k_pos[None, :], -MAX_REL, MAX_REL) + MAX_REL
    bias = rel_bias[:, rel_idx].astype(jnp.float32)
    key_chunk = jnp.arange(n_chunks)[:, None] - LEFT_CHUNKS + (k_pos // CHUNK)[None, :]
    valid = key_chunk >= 0
    s = jnp.where(valid[None, None, :, None, :], s + bias[:, None], -1e30)
    p = jax.nn.softmax(s, axis=-1)
    out = jnp.einsum('bhcqk,bckhd->bcqhd', p.astype(v.dtype), v_band)
    return out.reshape(B, S, H * dh)


def chunk_retention(q, k, v):
    B, S, H, dk = q.shape
    dv = v.shape[-1]
    n_chunks = S // CHUNK
    L = CHUNK
    f32 = jnp.float32
    log_g = jnp.log(1.0 - jnp.power(2.0, -5.0 - jnp.arange(H, dtype=f32)))
    pos = jnp.arange(L, dtype=f32)
    inner_decay = jnp.exp(log_g[:, None, None] * jnp.abs(pos[:, None] - pos[None, :]))
    qc = (rotary(q) * (dk ** -0.5)).reshape(B, n_chunks, L, H, dk)
    kc = rotary(k).reshape(B, n_chunks, L, H, dk)
    vc = v.astype(f32).reshape(B, n_chunks, L, H, dv)
    s = jnp.einsum('bclhd,bcmhd->bchlm', qc, kc) * inner_decay[None, None]
    inner = jnp.einsum('bchlm,bcmhe->bclhe', s, vc)
    k_decay = jnp.exp(log_g[None, :] * (L - 1 - pos)[:, None])
    chunk_kv = jnp.einsum('bclhd,bclhe->bchde', kc * k_decay[None, None, :, :, None], vc)
    g_chunk = jnp.exp(log_g * L)[None, :, None, None]

    def step(state, kv):
        return state * g_chunk + kv, state

    init = jnp.zeros((B, H, dk, dv), f32)
    _, prev = lax.scan(step, init, jnp.swapaxes(chunk_kv, 0, 1))
    prev = jnp.swapaxes(prev, 0, 1)
    q_decay = jnp.exp(log_g[None, :] * (pos + 1.0)[:, None])
    cross = jnp.einsum('bclhd,bchde->bclhe', qc, prev) * q_decay[None, None, :, :, None]
    out = (inner + cross).reshape(B, S, H, dv)
    mu = jnp.mean(out, -1, keepdims=True)
    var = jnp.mean(jnp.square(out - mu), -1, keepdims=True)
    out = (out - mu) * lax.rsqrt(var + LN_EPS)
    return out.reshape(B, S, H * dv)


def s5_mixer(u, lam_re, lam_im, log_step, b_re, b_im, c_re, c_im, d_skip, w_glu):
    B, S, _ = u.shape
    f32 = jnp.float32
    ug = u.astype(f32).reshape(B, S, SSM_GROUPS, SSM_GROUP)
    lam = lax.complex(lam_re.astype(f32), lam_im.astype(f32))
    step = jnp.exp(log_step.astype(f32))[:, None]
    lam_bar = jnp.exp(lam * step)
    b = lax.complex(b_re.astype(f32), b_im.astype(f32))
    b_bar = ((lam_bar - 1.0) / lam)[..., None] * b
    bu = jnp.einsum('gpk,bsgk->bsgp', b_bar, ug.astype(jnp.complex64))
    a = jnp.broadcast_to(lam_bar, bu.shape)

    def combine(e1, e2):
        a1, x1 = e1
        a2, x2 = e2
        return a1 * a2, a2 * x1 + x2

    _, states = lax.associative_scan(combine, (a, bu), axis=1)
    c = lax.complex(c_re.astype(f32), c_im.astype(f32))
    y = jnp.real(jnp.einsum('gkp,bsgp->bsgk', c, states)) + d_skip.astype(f32).reshape(SSM_GROUPS, SSM_GROUP) * ug
    y = jax.nn.gelu(y.reshape(B, S, SSM_WIDTH))
    y = y * jax.nn.sigmoid(y @ w_glu.astype(f32))
    return y.astype(u.dtype)


def conv_ffn(x, w_up, conv_w, conv_b, w_down):
    h = x @ w_up
    ch = h.shape[-1]
    h = lax.conv_general_dilated(h, conv_w[:, None, :], window_strides=(1,), padding=[(CONV_WIDTH - 1, 0)],
                                 dimension_numbers=('NWC', 'WIO', 'NWC'), feature_group_count=ch) + conv_b
    a, g = h[..., :D_FF], h[..., D_FF:]
    return (jax.nn.gelu(a) * g) @ w_down


def setup_inputs(seed: int = 0) -> dict:
    key = jax.random.key(seed)
    ks = jax.random.split(key, 32)
    f32 = jnp.float32
    nrm = lambda k, shape, scale: jax.random.normal(k, shape, f32) * scale
    L, D = DEPTH, D_MODEL
    G, P, K = SSM_GROUPS, SSM_STATE, SSM_GROUP
    inp = {}
    inp['x'] = jax.random.normal(ks[0], (BATCH, SEQ, D), f32)
    inp['ln_in_g'] = 1.0 + nrm(ks[1], (D,), 0.01)
    inp['ln_in_b'] = nrm(ks[2], (D,), 0.01)
    inp['w_in'] = nrm(ks[3], (L, D, IN_WIDTH), D ** -0.5)
    inp['rel_bias'] = nrm(ks[4], (L, ATT_HEADS, 2 * MAX_REL + 1), 0.1)
    inp['w_proj_a'] = nrm(ks[5], (L, ATT_WIDTH, D), ATT_WIDTH ** -0.5)
    inp['w_proj_b'] = nrm(ks[6], (L, RET_V_WIDTH, D), RET_V_WIDTH ** -0.5)
    inp['w_proj_c'] = nrm(ks[7], (L, SSM_WIDTH, D), SSM_WIDTH ** -0.5)
    inp['lam_re'] = -0.5 + nrm(ks[8], (L, G, P), 0.01)
    inp['lam_im'] = math.pi * jnp.broadcast_to(jnp.arange(P, dtype=f32), (L, G, P)) + nrm(ks[9], (L, G, P), 0.01)
    inp['log_step'] = jax.random.uniform(ks[10], (L, G), f32, math.log(DT_MIN), math.log(DT_MAX))
    inp['b_re'] = nrm(ks[11], (L, G, P, K), (2 * K) ** -0.5)
    inp['b_im'] = nrm(ks[12], (L, G, P, K), (2 * K) ** -0.5)
    inp['c_re'] = nrm(ks[13], (L, G, K, P), (2 * P) ** -0.5)
    inp['c_im'] = nrm(ks[14], (L, G, K, P), (2 * P) ** -0.5)
    inp['d_skip'] = nrm(ks[15], (L, SSM_WIDTH), 1.0)
    inp['w_glu'] = nrm(ks[16], (L, SSM_WIDTH, SSM_WIDTH), SSM_WIDTH ** -0.5)
    inp['w_o'] = nrm(ks[17], (L, D, D), D ** -0.5 * DEEPNORM_BETA)
    inp['ln1_g'] = 1.0 + nrm(ks[18], (L, D), 0.01)
    inp['ln1_b'] = nrm(ks[19], (L, D), 0.01)
    inp['w_up'] = nrm(ks[20], (L, D, 2 * D_FF), D ** -0.5)
    inp['conv_w'] = nrm(ks[21], (L, CONV_WIDTH, 2 * D_FF), CONV_WIDTH ** -0.5)
    inp['conv_b'] = nrm(ks[22], (L, 2 * D_FF), 0.01)
    inp['w_down'] = nrm(ks[23], (L, D_FF, D), D_FF ** -0.5 * DEEPNORM_BETA)
    inp['ln2_g'] = 1.0 + nrm(ks[24], (L, D), 0.01)
    inp['ln2_b'] = nrm(ks[25], (L, D), 0.01)
    return inp


def reference(x, ln_in_g, ln_in_b, w_in, rel_bias, w_proj_a, w_proj_b, w_proj_c, lam_re, lam_im, log_step,
              b_re, b_im, c_re, c_im, d_skip, w_glu, w_o, ln1_g, ln1_b, w_up, conv_w, conv_b, w_down,
              ln2_g, ln2_b):
    B, S, D = x.shape
    h = layer_norm(x, ln_in_g, ln_in_b)
    for l in range(DEPTH):
        w = w_in[l]
        qa = (h @ in_cols(w, 0)).reshape(B, S, ATT_HEADS, ATT_HEAD_DIM)
        ka = (h @ in_cols(w, 1)).reshape(B, S, ATT_HEADS, ATT_HEAD_DIM)
        va = (h @ in_cols(w, 2)).reshape(B, S, ATT_HEADS, ATT_HEAD_DIM)
        pa = band_attention(qa, ka, va, rel_bias[l]) @ w_proj_a[l]
        qb = (h @ in_cols(w, 3)).reshape(B, S, RET_HEADS, RET_QK_DIM)
        kb = (h @ in_cols(w, 4)).reshape(B, S, RET_HEADS, RET_QK_DIM)
        vb = (h @ in_cols(w, 5)).reshape(B, S, RET_HEADS, RET_V_DIM)
        ret = chunk_retention(qb, kb, vb).astype(h.dtype) * jax.nn.silu(h @ in_cols(w, 6))
        pb = ret @ w_proj_b[l]
        uc = h @ in_cols(w, 7)
        pc = s5_mixer(uc, lam_re[l], lam_im[l], log_step[l], b_re[l], b_im[l], c_re[l], c_im[l],
                      d_skip[l], w_glu[l]) @ w_proj_c[l]
        gates = jax.nn.sigmoid(h @ in_cols(w, 8)).reshape(B, S, N_BRANCH, D)
        merged = gates[:, :, 0] * pa + gates[:, :, 1] * pb + gates[:, :, 2] * pc
        h = layer_norm(DEEPNORM_ALPHA * h + merged @ w_o[l], ln1_g[l], ln1_b[l])
        h = layer_norm(DEEPNORM_ALPHA * h + conv_ffn(h, w_up[l], conv_w[l], conv_b[l], w_down[l]),
                       ln2_g[l], ln2_b[l])
    return h
```

```python
import functools
import math

import jax
import jax.numpy as jnp
from jax import lax
from jax.experimental import pallas as pl
from jax.experimental.pallas import tpu as pltpu

F32 = jnp.float32
BF16 = jnp.bfloat16

D_MODEL = 1024
DEPTH = 2
CHUNK = 64
ATT_HEADS = 8
ATT_HEAD_DIM = 64
ATT_WIDTH = ATT_HEADS * ATT_HEAD_DIM
LEFT_CHUNKS = 8
MAX_REL = 128
RET_HEADS = 8
RET_QK_DIM = 64
RET_V_DIM = 128
RET_QK_WIDTH = RET_HEADS * RET_QK_DIM
RET_V_WIDTH = RET_HEADS * RET_V_DIM
ROPE_BASE = 10000.0
SSM_WIDTH = 512
SSM_GROUP = 16
SSM_GROUPS = SSM_WIDTH // SSM_GROUP
SSM_STATE = 64
D_FF = 2816
N_BRANCH = 3
IN_WIDTH = 8192
DEEPNORM_ALPHA = (2.0 * DEPTH) ** 0.25
LN_EPS = 1e-5
MASK_VALUE = -1e30

LANES = 128
V7X_VMEM_BYTES = 64 * 1024 * 1024

ROW_TILE = 512
COL_CHUNK = 512
ATT_Q_BLOCK = 256
RET_BLOCK = 256
SSM_T_BLOCK = 64
SSM_LANE_TILES = SSM_WIDTH // LANES
SSM_TILE_STATE = (LANES // SSM_GROUP) * SSM_STATE
FF_CHUNK = 256


def _vmem_limit(nbytes):
    return int(min(nbytes, V7X_VMEM_BYTES - 6 * 1024 * 1024))


def _resident(shape):
    nd = len(shape)
    return pl.BlockSpec(shape, lambda *_: (0,) * nd, pipeline_mode=pl.Buffered(1))


def _layer_norm(x, g, b):
    mu = jnp.mean(x, -1, keepdims=True)
    xc = x - mu
    var = jnp.mean(xc * xc, -1, keepdims=True)
    return xc * lax.rsqrt(var + LN_EPS) * g + b


def _gelu(x):
    return 0.5 * x * (1.0 + jnp.tanh(math.sqrt(2.0 / math.pi) * (x + 0.044715 * (x * x * x))))


def _in_proj_kernel(apply_ln, *refs):
    if apply_ln:
        (x_ref, g_ref, b_ref, w_ref, cos_ref, sin_ref,
         hf_ref, att_ref, rqk_ref, rv_ref, rg_ref, u_ref, gate_ref) = refs
    else:
        (x_ref, w_ref, cos_ref, sin_ref,
         att_ref, rqk_ref, rv_ref, rg_ref, u_ref, gate_ref) = refs
    x = x_ref[...]
    if apply_ln:
        x = _layer_norm(x, g_ref[...], b_ref[...])
        hf_ref[...] = x
    xb = x.astype(BF16)

    def proj(c):
        return jnp.dot(xb, w_ref[:, c * COL_CHUNK:(c + 1) * COL_CHUNK], preferred_element_type=F32)

    lane = lax.broadcasted_iota(jnp.int32, (1, COL_CHUNK), 1)
    first_half = (lane % RET_QK_DIM) < (RET_QK_DIM // 2)
    half = RET_QK_DIM // 2

    def rotary(a):
        swapped = jnp.where(first_half, pltpu.roll(a, COL_CHUNK - half, 1), pltpu.roll(a, half, 1))
        return a * cos_ref[...] + swapped * sin_ref[...]

    att_ref[:, 0:512] = (proj(0) * (ATT_HEAD_DIM ** -0.5)).astype(BF16)
    att_ref[:, 512:1024] = proj(1).astype(BF16)
    att_ref[:, 1024:1536] = proj(2).astype(BF16)
    rqk_ref[:, 0:512] = (rotary(proj(3)) * (RET_QK_DIM ** -0.5)).astype(BF16)
    rqk_ref[:, 512:1024] = rotary(proj(4)).astype(BF16)
    rv_ref[:, 0:512] = proj(5).astype(BF16)
    rv_ref[:, 512:1024] = proj(6).astype(BF16)
    for c in range(2):
        a = proj(7 + c)
        rg_ref[:, c * 512:(c + 1) * 512] = (a * jax.nn.sigmoid(a)).astype(BF16)
    u_ref[...] = proj(9).astype(BF16)
    for c in range(6):
        gate_ref[:, c * 512:(c + 1) * 512] = jax.nn.sigmoid(proj(10 + c)).astype(BF16)


def _in_proj(x2d, ln_g, ln_b, w_bf16, cos_t, sin_t, batch, seq):
    m = x2d.shape[0]
    tm = ROW_TILE
    n_s = seq // tm
    apply_ln = ln_g is not None
    row = lambda w: pl.BlockSpec((tm, w), lambda i: (i, 0))
    in_specs = [row(D_MODEL)]
    args = [x2d]
    if apply_ln:
        in_specs += [_resident((1, D_MODEL)), _resident((1, D_MODEL))]
        args += [ln_g, ln_b]
    in_specs += [_resident((D_MODEL, IN_WIDTH)),
                 pl.BlockSpec((tm, COL_CHUNK), lambda i: (i % n_s, 0)),
                 pl.BlockSpec((tm, COL_CHUNK), lambda i: (i % n_s, 0))]
    args += [w_bf16, cos_t, sin_t]
    out_shape, out_specs = [], []
    if apply_ln:
        out_shape.append(jax.ShapeDtypeStruct((m, D_MODEL), F32))
        out_specs.append(row(D_MODEL))
    for w in (3 * ATT_WIDTH, 2 * RET_QK_WIDTH, RET_V_WIDTH, RET_V_WIDTH):
        out_shape.append(jax.ShapeDtypeStruct((m, w), BF16))
        out_specs.append(row(w))
    out_shape.append(jax.ShapeDtypeStruct((seq, batch * SSM_WIDTH), BF16))
    out_specs.append(pl.BlockSpec((tm, SSM_WIDTH), lambda i: (i % n_s, i // n_s)))
    out_shape.append(jax.ShapeDtypeStruct((m, N_BRANCH * D_MODEL), BF16))
    out_specs.append(row(N_BRANCH * D_MODEL))
    return pl.pallas_call(
        functools.partial(_in_proj_kernel, apply_ln),
        grid=(m // tm,),
        in_specs=in_specs,
        out_specs=out_specs,
        out_shape=out_shape,
        compiler_params=pltpu.CompilerParams(
            dimension_semantics=("arbitrary",), vmem_limit_bytes=_vmem_limit(56 * 1024 * 1024)),
        name="in_proj_ln" if apply_ln else "in_proj",
    )(*args)


def _att_kernel(q_ref, k0_ref, k1_ref, k2_ref, v0_ref, v1_ref, v2_ref, bias_ref, o_ref):
    qi = pl.program_id(1)
    tq = ATT_Q_BLOCK
    q = q_ref[...]
    k = jnp.concatenate([k0_ref[...], k1_ref[...], k2_ref[...]], axis=0)
    v = jnp.concatenate([v0_ref[...], v1_ref[...], v2_ref[...]], axis=0)
    col = lax.broadcasted_iota(jnp.int32, (1, 3 * tq), 1)
    pen = jnp.where(col < tq, jnp.where(qi >= 2, 0.0, MASK_VALUE),
                    jnp.where(col < 2 * tq, jnp.where(qi >= 1, 0.0, MASK_VALUE), 0.0))
    lane = lax.broadcasted_iota(jnp.int32, (1, LANES), 1)
    for p in range(ATT_WIDTH // LANES):
        qp = q[:, p * LANES:(p + 1) * LANES]
        kp = k[:, p * LANES:(p + 1) * LANES]
        vp = v[:, p * LANES:(p + 1) * LANES]
        outs = []
        for hh in range(2):
            h = 2 * p + hh
            head_lanes = (lane // ATT_HEAD_DIM) == hh
            qm = qp * head_lanes.astype(BF16)
            s = lax.dot_general(qm, kp, (((1,), (1,)), ((), ())), preferred_element_type=F32)
            s = s + bias_ref[h] + pen
            mx = jnp.max(s, -1, keepdims=True)
            e = jnp.exp(s - mx)
            den = jnp.sum(e, -1, keepdims=True)
            o = jnp.dot(e.astype(BF16), vp, preferred_element_type=F32)
            outs.append(o / den)
        o_ref[:, p * LANES:(p + 1) * LANES] = jnp.where(
            (lane // ATT_HEAD_DIM) == 0, outs[0], outs[1]).astype(BF16)


def _att_bias_table(rel_bias):
    tq = ATT_Q_BLOCK
    q = jnp.arange(tq)[:, None]
    kk = jnp.arange(3 * tq)[None, :]
    rel = jnp.clip(q - kk + 2 * tq, -MAX_REL, MAX_REL) + MAX_REL
    qc, kc = q // CHUNK, kk // CHUNK
    ok = (kc >= qc) & (kc <= qc + LEFT_CHUNKS)
    return jnp.where(ok[None], rel_bias[:, rel].astype(F32), MASK_VALUE)


def _attention(att_qkv, bias_tab, batch, seq):
    m = att_qkv.shape[0]
    tq = ATT_Q_BLOCK
    assert LEFT_CHUNKS * CHUNK == 2 * tq and seq % tq == 0
    nq = seq // tq
    qspec = pl.BlockSpec((tq, ATT_WIDTH), lambda b, i: (b * nq + i, 0))

    def kv(col, back):
        return pl.BlockSpec((tq, ATT_WIDTH), lambda b, i: (b * nq + jnp.maximum(i - back, 0), col))

    return pl.pallas_call(
        _att_kernel,
        grid=(batch, nq),
        in_specs=[qspec, kv(1, 2), kv(1, 1), kv(1, 0), kv(2, 2), kv(2, 1), kv(2, 0),
                  _resident((ATT_HEADS, tq, 3 * tq))],
        out_specs=pl.BlockSpec((tq, ATT_WIDTH), lambda b, i: (b * nq + i, 0)),
        out_shape=jax.ShapeDtypeStruct((m, ATT_WIDTH), BF16),
        compiler_params=pltpu.CompilerParams(
            dimension_semantics=("arbitrary", "arbitrary"),
            vmem_limit_bytes=_vmem_limit(40 * 1024 * 1024)),
        name="band_attention",
    )(att_qkv, att_qkv, att_qkv, att_qkv, att_qkv, att_qkv, att_qkv, bias_tab)


def _ret_kernel(q_ref, k_ref, v_ref, g_ref, dm_ref, qdec_ref, kdec_ref, gblk_ref, o_ref, state_ref):
    @pl.when(pl.program_id(1) == 0)
    def _():
        state_ref[...] = jnp.zeros_like(state_ref)

    q = q_ref[...]
    k = k_ref[...]
    kd = k.astype(F32) * kdec_ref[...]
    lane = lax.broadcasted_iota(jnp.int32, (1, LANES), 1)
    sub = lax.broadcasted_iota(jnp.int32, (LANES, 1), 0)
    for p in range(RET_QK_WIDTH // LANES):
        qp = q[:, p * LANES:(p + 1) * LANES]
        kp = k[:, p * LANES:(p + 1) * LANES]
        kd_t = kd[:, p * LANES:(p + 1) * LANES].T
        st = state_ref[p]
        st_b = st.astype(BF16)
        new_st = st * gblk_ref[p]
        for hh in range(2):
            h = 2 * p + hh
            qm = qp * ((lane // RET_QK_DIM) == hh).astype(BF16)
            s = lax.dot_general(qm, kp, (((1,), (1,)), ((), ())), preferred_element_type=F32)
            sd = (s * dm_ref[h]).astype(BF16)
            vh = v_ref[:, h * RET_V_DIM:(h + 1) * RET_V_DIM]
            inner = jnp.dot(sd, vh, preferred_element_type=F32)
            cross = jnp.dot(qm, st_b, preferred_element_type=F32) * qdec_ref[h]
            out = inner + cross
            mu = jnp.mean(out, -1, keepdims=True)
            oc = out - mu
            var = jnp.mean(oc * oc, -1, keepdims=True)
            nrm = oc * lax.rsqrt(var + LN_EPS)
            gate = g_ref[:, h * RET_V_DIM:(h + 1) * RET_V_DIM].astype(F32)
            o_ref[:, h * RET_V_DIM:(h + 1) * RET_V_DIM] = (nrm * gate).astype(BF16)
            kd_h = jnp.where((sub // RET_QK_DIM) == hh, kd_t, 0.0).astype(BF16)
            new_st = new_st + jnp.dot(kd_h, vh, preferred_element_type=F32)
        state_ref[p] = new_st


def _ret_tables():
    t = RET_BLOCK
    log_g = jnp.log(1.0 - jnp.power(2.0, -5.0 - jnp.arange(RET_HEADS, dtype=F32)))
    pos = jnp.arange(t, dtype=F32)
    n, mm = pos[:, None], pos[None, :]
    cn, cm = jnp.floor(n / CHUNK), jnp.floor(mm / CHUNK)
    expo = jnp.where(cn == cm, jnp.abs(n - mm), n - mm)
    dm = jnp.where((cm <= cn)[None], jnp.exp(log_g[:, None, None] * jnp.where(cm <= cn, expo, 0.0)[None]), 0.0)
    qdec = jnp.exp(log_g[:, None] * (pos + 1.0)[None, :])
    qdec = jnp.broadcast_to(qdec[:, :, None], (RET_HEADS, t, RET_V_DIM))
    kdec = jnp.exp(log_g[:, None] * (t - 1.0 - pos)[None, :])
    kdec = jnp.repeat(kdec.T, RET_QK_DIM, axis=1)
    gblk = jnp.repeat(jnp.exp(log_g * t), RET_QK_DIM).reshape(RET_QK_WIDTH // LANES, LANES, 1)
    gblk = jnp.broadcast_to(gblk, (RET_QK_WIDTH // LANES, LANES, RET_V_DIM))
    return dm, qdec, kdec, gblk


def _retention(ret_qk, ret_v, ret_g, tables, batch, seq):
    m = ret_qk.shape[0]
    t = RET_BLOCK
    assert seq % t == 0 and t % CHUNK == 0
    nb = seq // t
    dm, qdec, kdec, gblk = tables
    n_pairs = RET_QK_WIDTH // LANES
    rows = lambda w, col: pl.BlockSpec((t, w), lambda b, i: (b * nb + i, col))
    return pl.pallas_call(
        _ret_kernel,
        grid=(batch, nb),
        in_specs=[rows(RET_QK_WIDTH, 0), rows(RET_QK_WIDTH, 1), rows(RET_V_WIDTH, 0), rows(RET_V_WIDTH, 0),
                  _resident((RET_HEADS, t, t)), _resident((RET_HEADS, t, RET_V_DIM)),
                  _resident((t, RET_QK_WIDTH)), _resident((n_pairs, LANES, RET_V_DIM))],
        out_specs=rows(RET_V_WIDTH, 0),
        out_shape=jax.ShapeDtypeStruct((m, RET_V_WIDTH), BF16),
        scratch_shapes=[pltpu.VMEM((n_pairs, LANES, RET_V_DIM), F32)],
        compiler_params=pltpu.CompilerParams(
            dimension_semantics=("arbitrary", "arbitrary"),
            vmem_limit_bytes=_vmem_limit(40 * 1024 * 1024)),
        name="chunk_retention",
    )(ret_qk, ret_qk, ret_v, ret_g, dm, qdec, kdec, gblk)


def _s5_disc_kernel(lre_ref, lim_ref, lstep_ref, bre_ref, bim_ref, lbr_ref, lbi_ref, bbr_ref, bbi_ref):
    lre, lim = lre_ref[...], lim_ref[...]
    step = jnp.exp(lstep_ref[...])
    mag = jnp.exp(lre * step)
    ang = lim * step
    lbr = mag * jnp.cos(ang)
    lbi = mag * jnp.sin(ang)
    den = lre * lre + lim * lim
    nr = lbr - 1.0
    cr = (nr * lre + lbi * lim) / den
    ci = (lbi * lre - nr * lim) / den
    lbr_ref[...] = lbr
    lbi_ref[...] = lbi
    bre, bim = bre_ref[...], bim_ref[...]
    bbr_ref[...] = cr * bre - ci * bim
    bbi_ref[...] = cr * bim + ci * bre


def _s5_discretise(lam_re, lam_im, log_step, b_re, b_im):
    gp = SSM_GROUPS * SSM_STATE
    row = lambda a: a.reshape(1, gp).astype(F32)
    lstep = jnp.repeat(log_step.astype(F32), SSM_STATE).reshape(1, gp)
    b_t = lambda b: b.astype(F32).reshape(gp, SSM_GROUP).T
    full = lambda s: pl.BlockSpec(s, lambda: (0,) * len(s))
    return pl.pallas_call(
        _s5_disc_kernel,
        in_specs=[full((1, gp))] * 3 + [full((SSM_GROUP, gp))] * 2,
        out_specs=[full((1, gp))] * 2 + [full((SSM_GROUP, gp))] * 2,
        out_shape=[jax.ShapeDtypeStruct((1, gp), F32)] * 2 + [jax.ShapeDtypeStruct((SSM_GROUP, gp), F32)] * 2,
        name="s5_discretise",
    )(row(lam_re), row(lam_im), lstep, b_t(b_re), b_t(b_im))


def _s5_matrices(lbr, lbi, bbr, bbi, c_re, c_im, batch):
    nt, gl = SSM_LANE_TILES, LANES // SSM_GROUP
    eye = jnp.eye(gl, dtype=F32)

    def expand(bb):
        bb = bb.reshape(SSM_GROUP, nt, gl, SSM_STATE).transpose(1, 2, 0, 3)
        return jnp.einsum('tgip,gh->tgihp', bb, eye).reshape(nt, LANES, SSM_TILE_STATE)

    def contract(c):
        c = c.astype(F32).reshape(nt, gl, SSM_GROUP, SSM_STATE)
        return jnp.einsum('tgop,gh->tgpho', c, eye).reshape(nt, SSM_TILE_STATE, LANES)

    b_big = jnp.concatenate([expand(bbr), expand(bbi)], axis=-1).astype(BF16)
    c_big = jnp.concatenate([contract(c_re), -contract(c_im)], axis=1).astype(BF16)
    lam_r = jnp.broadcast_to(lbr, (batch, SSM_GROUPS * SSM_STATE))
    lam_i = jnp.broadcast_to(lbi, (batch, SSM_GROUPS * SSM_STATE))
    return b_big, c_big, lam_r, lam_i


def _s5_kernel(batch, u_ref, bbig_ref, cbig_ref, lr_ref, li_ref, dskip_ref, wglu_ref, y_ref, x_ref, st_ref):
    @pl.when(pl.program_id(0) == 0)
    def _():
        st_ref[...] = jnp.zeros_like(st_ref)

    ts = SSM_TILE_STATE
    u = u_ref[...]
    for kt in range(SSM_LANE_TILES):
        x_ref[:, 2 * ts * kt:2 * ts * (kt + 1)] = jnp.dot(
            u[:, kt * LANES:(kt + 1) * LANES], bbig_ref[kt], preferred_element_type=F32)

    for kt in range(SSM_LANE_TILES):
        re0, im0 = 2 * ts * kt, 2 * ts * kt + ts
        lr = lr_ref[:, kt * ts:(kt + 1) * ts]
        li = li_ref[:, kt * ts:(kt + 1) * ts]

        def step(t, carry, re0=re0, im0=im0, lr=lr, li=li):
            xr, xi = carry
            rows = pl.ds(pl.multiple_of(t * batch, batch), batch)
            nr = lr * xr - li * xi + x_ref[rows, re0:re0 + ts]
            ni = lr * xi + li * xr + x_ref[rows, im0:im0 + ts]
            x_ref[rows, re0:re0 + ts] = nr
            x_ref[rows, im0:im0 + ts] = ni
            return nr, ni

        xr, xi = lax.fori_loop(0, SSM_T_BLOCK, step,
                               (st_ref[:, re0:re0 + ts], st_ref[:, im0:im0 + ts]), unroll=4)
        st_ref[:, re0:re0 + ts] = xr
        st_ref[:, im0:im0 + ts] = xi

    ys = []
    for kt in range(SSM_LANE_TILES):
        xs = x_ref[:, 2 * ts * kt:2 * ts * (kt + 1)].astype(BF16)
        ys.append(jnp.dot(xs, cbig_ref[kt], preferred_element_type=F32))
    y = jnp.concatenate(ys, axis=-1) + dskip_ref[...] * u.astype(F32)
    y = _gelu(y)
    y = y * jax.nn.sigmoid(jnp.dot(y.astype(BF16), wglu_ref[...], preferred_element_type=F32))
    y_ref[...] = y.astype(BF16)


def _s5(u_tb, mats, d_skip, w_glu, batch, seq):
    b_big, c_big, lam_r, lam_i = mats
    rows = SSM_T_BLOCK * batch
    assert seq % SSM_T_BLOCK == 0 and batch % 8 == 0
    n_state = 2 * SSM_TILE_STATE * SSM_LANE_TILES
    blk = pl.BlockSpec((rows, SSM_WIDTH), lambda j: (j, 0))
    return pl.pallas_call(
        functools.partial(_s5_kernel, batch),
        grid=(seq // SSM_T_BLOCK,),
        in_specs=[blk, _resident(b_big.shape), _resident(c_big.shape), _resident(lam_r.shape),
                  _resident(lam_i.shape), _resident((1, SSM_WIDTH)), _resident((SSM_WIDTH, SSM_WIDTH))],
        out_specs=blk,
        out_shape=jax.ShapeDtypeStruct((seq * batch, SSM_WIDTH), BF16),
        scratch_shapes=[pltpu.VMEM((rows, n_state), F32), pltpu.VMEM((batch, n_state), F32)],
        compiler_params=pltpu.CompilerParams(
            dimension_semantics=("arbitrary",), vmem_limit_bytes=_vmem_limit(40 * 1024 * 1024)),
        name="s5_scan",
    )(u_tb, b_big, c_big, lam_r, lam_i, d_skip, w_glu)


def _merge_kernel(att_ref, ret_ref, yc_ref, gate_ref, h_ref, wa_ref, wb_ref, wc_ref, wo_ref,
                  g_ref, b_ref, out_ref):
    pa = jnp.dot(att_ref[...], wa_ref[...], preferred_element_type=F32)
    pb = jnp.dot(ret_ref[...], wb_ref[...], preferred_element_type=F32)
    pc = jnp.dot(yc_ref[...], wc_ref[...], preferred_element_type=F32)
    d = D_MODEL
    merged = (gate_ref[:, 0:d].astype(F32) * pa + gate_ref[:, d:2 * d].astype(F32) * pb
              + gate_ref[:, 2 * d:3 * d].astype(F32) * pc)
    y = DEEPNORM_ALPHA * h_ref[...] + jnp.dot(merged.astype(BF16), wo_ref[...], preferred_element_type=F32)
    out_ref[...] = _layer_norm(y, g_ref[...], b_ref[...])


def _merge(att_o, ret_o, yc_tb, gates, h, wa, wb, wc, wo, ln_g, ln_b, batch, seq):
    m = h.shape[0]
    tm = ROW_TILE
    n_s = seq // tm
    row = lambda w: pl.BlockSpec((tm, w), lambda i: (i, 0))
    return pl.pallas_call(
        _merge_kernel,
        grid=(m // tm,),
        in_specs=[row(ATT_WIDTH), row(RET_V_WIDTH),
                  pl.BlockSpec((tm, SSM_WIDTH), lambda i: (i % n_s, i // n_s)),
                  row(N_BRANCH * D_MODEL), row(D_MODEL),
                  _resident(wa.shape), _resident(wb.shape), _resident(wc.shape), _resident(wo.shape),
                  _resident((1, D_MODEL)), _resident((1, D_MODEL))],
        out_specs=row(D_MODEL),
        out_shape=jax.ShapeDtypeStruct((m, D_MODEL), F32),
        compiler_params=pltpu.CompilerParams(
            dimension_semantics=("arbitrary",), vmem_limit_bytes=_vmem_limit(40 * 1024 * 1024)),
        name="gated_merge",
    )(att_o, ret_o, yc_tb.reshape(seq, batch * SSM_WIDTH), gates, h, wa, wb, wc, wo, ln_g, ln_b)


FF_HALO = 8


def _ffn_kernel(n_s, h_ref, wup_ref, cw_ref, cb_ref, wdn_ref, g_ref, b_ref, out_ref, ext_ref, acc_ref):
    tm = ROW_TILE
    i = pl.program_id(0)

    @pl.when(i % n_s == 0)
    def _():
        ext_ref[0:FF_HALO, :] = jnp.zeros((FF_HALO, 2 * D_FF), F32)

    @pl.when(i % n_s != 0)
    def _():
        ext_ref[0:FF_HALO, :] = ext_ref[tm:tm + FF_HALO, :]

    h = h_ref[...]
    hb = h.astype(BF16)
    for c in range(2 * D_FF // COL_CHUNK):
        cols = slice(c * COL_CHUNK, (c + 1) * COL_CHUNK)
        ext_ref[FF_HALO:FF_HALO + tm, cols] = jnp.dot(hb, wup_ref[:, cols], preferred_element_type=F32)

    def conv(col):
        cols = slice(col, col + FF_CHUNK)
        w = cw_ref[:, cols]
        return (ext_ref[FF_HALO - 2:FF_HALO - 2 + tm, cols] * w[0:1]
                + ext_ref[FF_HALO - 1:FF_HALO - 1 + tm, cols] * w[1:2]
                + ext_ref[FF_HALO:FF_HALO + tm, cols] * w[2:3] + cb_ref[:, cols])

    for j in range(D_FF // FF_CHUNK):
        act = (_gelu(conv(j * FF_CHUNK)) * conv(D_FF + j * FF_CHUNK)).astype(BF16)
        part = jnp.dot(act, wdn_ref[j * FF_CHUNK:(j + 1) * FF_CHUNK, :], preferred_element_type=F32)
        if j == 0:
            acc_ref[...] = part
        else:
            acc_ref[...] += part
    out_ref[...] = _layer_norm(DEEPNORM_ALPHA * h + acc_ref[...], g_ref[...], b_ref[...])


def _ffn(h, w_up, conv_w, conv_b, w_down, ln_g, ln_b, seq):
    m = h.shape[0]
    tm = ROW_TILE
    assert (2 * D_FF) % COL_CHUNK == 0 and D_FF % FF_CHUNK == 0
    row = pl.BlockSpec((tm, D_MODEL), lambda i: (i, 0))
    return pl.pallas_call(
        functools.partial(_ffn_kernel, seq // tm),
        grid=(m // tm,),
        in_specs=[row, _resident(w_up.shape), _resident(conv_w.shape), _resident((1, 2 * D_FF)),
                  _resident(w_down.shape), _resident((1, D_MODEL)), _resident((1, D_MODEL))],
        out_specs=row,
        out_shape=jax.ShapeDtypeStruct((m, D_MODEL), F32),
        scratch_shapes=[pltpu.VMEM((tm + 2 * FF_HALO, 2 * D_FF), F32), pltpu.VMEM((tm, D_MODEL), F32)],
        compiler_params=pltpu.CompilerParams(
            dimension_semantics=("arbitrary",), vmem_limit_bytes=_vmem_limit(56 * 1024 * 1024)),
        name="conv_ffn",
    )(h, w_up, conv_w, conv_b, w_down, ln_g, ln_b)


def _rotary_tables(seq):
    half = RET_QK_DIM // 2
    inv = ROPE_BASE ** (-jnp.arange(0, RET_QK_DIM, 2, dtype=F32) / RET_QK_DIM)
    ang = jnp.arange(seq, dtype=F32)[:, None] * inv[None, :]
    cos, sin = jnp.cos(ang), jnp.sin(ang)
    assert cos.shape[1] == half
    cos_t = jnp.tile(jnp.concatenate([cos, cos], -1), (1, RET_HEADS))
    sin_t = jnp.tile(jnp.concatenate([-sin, sin], -1), (1, RET_HEADS))
    return cos_t, sin_t


def kernel(x, ln_in_g, ln_in_b, w_in, rel_bias, w_proj_a, w_proj_b, w_proj_c, lam_re, lam_im, log_step,
           b_re, b_im, c_re, c_im, d_skip, w_glu, w_o, ln1_g, ln1_b, w_up, conv_w, conv_b, w_down,
           ln2_g, ln2_b):
    batch, seq, d = x.shape
    assert d == D_MODEL and seq % ROW_TILE == 0 and w_in.shape[0] == DEPTH
    m = batch * seq
    vec = lambda a: a.reshape(1, -1).astype(F32)
    cos_t, sin_t = _rotary_tables(seq)
    ret_tabs = _ret_tables()
    h = x.reshape(m, d)
    for l in range(DEPTH):
        w_l = w_in[l].astype(BF16)
        if l == 0:
            h, att_qkv, ret_qk, ret_v, ret_g, u_tb, gates = _in_proj(
                h, vec(ln_in_g), vec(ln_in_b), w_l, cos_t, sin_t, batch, seq)
        else:
            att_qkv, ret_qk, ret_v, ret_g, u_tb, gates = _in_proj(
                h, None, None, w_l, cos_t, sin_t, batch, seq)
        att_o = _attention(att_qkv, _att_bias_table(rel_bias[l]), batch, seq)
        ret_o = _retention(ret_qk, ret_v, ret_g, ret_tabs, batch, seq)
        lbr, lbi, bbr, bbi = _s5_discretise(lam_re[l], lam_im[l], log_step[l], b_re[l], b_im[l])
        mats = _s5_matrices(lbr, lbi, bbr, bbi, c_re[l], c_im[l], batch)
        yc_tb = _s5(u_tb.reshape(seq * batch, SSM_WIDTH), mats, vec(d_skip[l]), w_glu[l].astype(BF16),
                    batch, seq)
        h = _merge(att_o, ret_o, yc_tb, gates, h, w_proj_a[l].astype(BF16), w_proj_b[l].astype(BF16),
                   w_proj_c[l].astype(BF16), w_o[l].astype(BF16), vec(ln1_g[l]), vec(ln1_b[l]), batch, seq)
        h = _ffn(h, w_up[l].astype(BF16), conv_w[l].astype(F32), vec(conv_b[l]), w_down[l].astype(BF16),
                 vec(ln2_g[l]), vec(ln2_b[l]), seq)
    return h.reshape(batch, seq, d)
```

```python
import functools
import math

import jax
import jax.numpy as jnp
import numpy as np
from jax import lax
from jax.experimental import pallas as pl
from jax.experimental.pallas import tpu as pltpu

F32 = jnp.float32
BF16 = jnp.bfloat16

D_MODEL = 1024
DEPTH = 2
CHUNK = 64
ATT_HEADS = 8
ATT_HEAD_DIM = 64
ATT_WIDTH = ATT_HEADS * ATT_HEAD_DIM
LEFT_CHUNKS = 8
MAX_REL = 128
RET_HEADS = 8
RET_QK_DIM = 64
RET_V_DIM = 128
RET_QK_WIDTH = RET_HEADS * RET_QK_DIM
RET_V_WIDTH = RET_HEADS * RET_V_DIM
ROPE_BASE = 10000.0
SSM_WIDTH = 512
SSM_GROUP = 16
SSM_GROUPS = SSM_WIDTH // SSM_GROUP
SSM_STATE = 64
D_FF = 2816
N_BRANCH = 3
IN_WIDTH = 8192
DEEPNORM_ALPHA = (2.0 * DEPTH) ** 0.25
LN_EPS = 1e-5
MASK_VALUE = -1e30
LOG2E = math.log2(math.e)

LANES = 128
V7X_VMEM_BYTES = 64 * 1024 * 1024

ROW_TILE = 512
COL_CHUNK = 512
ATT_Q_BLOCK = 256
RET_BLOCK = 256
SSM_T_BLOCK = 64
SSM_LANE_TILES = SSM_WIDTH // LANES
SSM_TILE_STATE = (LANES // SSM_GROUP) * SSM_STATE
FF_CHUNK = 256


def _vmem_limit(nbytes):
    return int(min(nbytes, V7X_VMEM_BYTES - 6 * 1024 * 1024))


def _resident(shape):
    nd = len(shape)
    return pl.BlockSpec(shape, lambda *_: (0,) * nd, pipeline_mode=pl.Buffered(1))


def _layer_norm(x, g, b):
    mu = jnp.mean(x, -1, keepdims=True)
    xc = x - mu
    var = jnp.mean(xc * xc, -1, keepdims=True)
    return xc * lax.rsqrt(var + LN_EPS) * g + b


def _gelu(x):
    return 0.5 * x * (1.0 + jnp.tanh(math.sqrt(2.0 / math.pi) * (x + 0.044715 * (x * x * x))))


def _in_proj_kernel(apply_ln, *refs):
    if apply_ln:
        (x_ref, g_ref, b_ref, w_ref, cos_ref, sin_ref,
         hf_ref, att_ref, rqk_ref, rv_ref, rg_ref, u_ref, gate_ref) = refs
    else:
        (x_ref, w_ref, cos_ref, sin_ref,
         att_ref, rqk_ref, rv_ref, rg_ref, u_ref, gate_ref) = refs
    x = x_ref[...]
    if apply_ln:
        x = _layer_norm(x, g_ref[...], b_ref[...])
        hf_ref[...] = x
    xb = x.astype(BF16)

    def proj(c):
        return jnp.dot(xb, w_ref[:, c * COL_CHUNK:(c + 1) * COL_CHUNK], preferred_element_type=F32)

    lane = lax.broadcasted_iota(jnp.int32, (1, COL_CHUNK), 1)
    first_half = (lane % RET_QK_DIM) < (RET_QK_DIM // 2)
    half = RET_QK_DIM // 2

    def rotary(a):
        swapped = jnp.where(first_half, pltpu.roll(a, COL_CHUNK - half, 1), pltpu.roll(a, half, 1))
        return a * cos_ref[...] + swapped * sin_ref[...]

    att_ref[:, 0:512] = (proj(0) * (ATT_HEAD_DIM ** -0.5 * LOG2E)).astype(BF16)
    att_ref[:, 512:1024] = proj(1).astype(BF16)
    att_ref[:, 1024:1536] = proj(2).astype(BF16)
    rqk_ref[:, 0:512] = (rotary(proj(3)) * (RET_QK_DIM ** -0.5)).astype(BF16)
    rqk_ref[:, 512:1024] = rotary(proj(4)).astype(BF16)
    rv_ref[:, 0:512] = proj(5).astype(BF16)
    rv_ref[:, 512:1024] = proj(6).astype(BF16)
    for c in range(2):
        a = proj(7 + c)
        rg_ref[:, c * 512:(c + 1) * 512] = (a * jax.nn.sigmoid(a)).astype(BF16)
    u_ref[...] = proj(9).astype(BF16)
    for c in range(6):
        gate_ref[:, c * 512:(c + 1) * 512] = jax.nn.sigmoid(proj(10 + c)).astype(BF16)


def _in_proj(x2d, ln_g, ln_b, w_bf16, cos_t, sin_t, batch, seq):
    m = x2d.shape[0]
    tm = ROW_TILE
    n_s = seq // tm
    apply_ln = ln_g is not None
    row = lambda w: pl.BlockSpec((tm, w), lambda i: (i, 0))
    in_specs = [row(D_MODEL)]
    args = [x2d]
    if apply_ln:
        in_specs += [_resident((1, D_MODEL)), _resident((1, D_MODEL))]
        args += [ln_g, ln_b]
    in_specs += [_resident((D_MODEL, IN_WIDTH)),
                 pl.BlockSpec((tm, COL_CHUNK), lambda i: (i % n_s, 0)),
                 pl.BlockSpec((tm, COL_CHUNK), lambda i: (i % n_s, 0))]
    args += [w_bf16, cos_t, sin_t]
    out_shape, out_specs = [], []
    if apply_ln:
        out_shape.append(jax.ShapeDtypeStruct((m, D_MODEL), F32))
        out_specs.append(row(D_MODEL))
    for w in (3 * ATT_WIDTH, 2 * RET_QK_WIDTH, RET_V_WIDTH, RET_V_WIDTH):
        out_shape.append(jax.ShapeDtypeStruct((m, w), BF16))
        out_specs.append(row(w))
    out_shape.append(jax.ShapeDtypeStruct((seq, batch * SSM_WIDTH), BF16))
    out_specs.append(pl.BlockSpec((tm, SSM_WIDTH), lambda i: (i % n_s, i // n_s)))
    out_shape.append(jax.ShapeDtypeStruct((m, N_BRANCH * D_MODEL), BF16))
    out_specs.append(row(N_BRANCH * D_MODEL))
    return pl.pallas_call(
        functools.partial(_in_proj_kernel, apply_ln),
        grid=(m // tm,),
        in_specs=in_specs,
        out_specs=out_specs,
        out_shape=out_shape,
        compiler_params=pltpu.CompilerParams(
            dimension_semantics=("arbitrary",), vmem_limit_bytes=_vmem_limit(56 * 1024 * 1024)),
        name="in_proj_ln" if apply_ln else "in_proj",
    )(*args)


def _att_kernel(q_ref, k0_ref, k1_ref, k2_ref, v0_ref, v1_ref, v2_ref, bias_ref, o_ref):
    qi = pl.program_id(1)
    tq = ATT_Q_BLOCK
    q = q_ref[...]
    k = jnp.concatenate([k0_ref[...], k1_ref[...], k2_ref[...]], axis=0)
    v = jnp.concatenate([v0_ref[...], v1_ref[...], v2_ref[...]], axis=0)
    lane = lax.broadcasted_iota(jnp.int32, (1, LANES), 1)
    first = (lane == 0).astype(BF16)
    row = lax.broadcasted_iota(jnp.int32, (3 * tq, 1), 0)
    pen = jnp.where(row < tq, jnp.where(qi >= 2, 0.0, MASK_VALUE),
                    jnp.where(row < 2 * tq, jnp.where(qi >= 1, 0.0, MASK_VALUE), 0.0))
    k_pen = pen.astype(BF16) * first
    q_one = jnp.broadcast_to(first, (tq, LANES))
    v_ones = jnp.ones((3 * tq, LANES), BF16)
    for p in range(ATT_WIDTH // LANES):
        qp = q[:, p * LANES:(p + 1) * LANES]
        k_ext = jnp.concatenate([k[:, p * LANES:(p + 1) * LANES], k_pen], axis=1)
        v_ext = jnp.concatenate([v[:, p * LANES:(p + 1) * LANES], v_ones], axis=1)
        outs = []
        for hh in range(2):
            h = 2 * p + hh
            head_lanes = ((lane // ATT_HEAD_DIM) == hh).astype(BF16)
            q_ext = jnp.concatenate([qp * head_lanes, q_one], axis=1)
            s = lax.dot_general(q_ext, k_ext, (((1,), (1,)), ((), ())), preferred_element_type=F32)
            s = s + bias_ref[h]
            e = jnp.exp2(s - jnp.max(s, -1, keepdims=True))
            o = jnp.dot(e.astype(BF16), v_ext, preferred_element_type=F32)
            outs.append(o[:, :LANES] / o[:, LANES:])
        o_ref[:, p * LANES:(p + 1) * LANES] = jnp.where(
            (lane // ATT_HEAD_DIM) == 0, outs[0], outs[1]).astype(BF16)


def _att_band_mask():
    tq = ATT_Q_BLOCK
    qc = (np.arange(tq) // CHUNK)[:, None]
    kc = (np.arange(3 * tq) // CHUNK)[None, :]
    return (kc >= qc) & (kc <= qc + LEFT_CHUNKS)


def _att_bias_table(rel_bias):
    tq = ATT_Q_BLOCK
    period = 4 * tq
    rb = rel_bias.astype(F32) * LOG2E
    nh = rb.shape[0]
    fill = lambda col, n: jnp.broadcast_to(rb[:, col:col + 1], (nh, n))
    g = jnp.concatenate([fill(2 * MAX_REL, 2 * tq - MAX_REL), rb[:, ::-1],
                         fill(0, tq - MAX_REL - 1), fill(2 * MAX_REL, tq)], axis=1)
    assert g.shape[1] == period
    toep = jnp.tile(g, (1, tq))[:, :tq * (period - 1)].reshape(nh, tq, period - 1)[:, :, :3 * tq]
    return jnp.where(_att_band_mask()[None], toep, MASK_VALUE)


def _attention(att_qkv, bias_tab, batch, seq):
    m = att_qkv.shape[0]
    tq = ATT_Q_BLOCK
    assert LEFT_CHUNKS * CHUNK == 2 * tq and seq % tq == 0
    nq = seq // tq
    qspec = pl.BlockSpec((tq, ATT_WIDTH), lambda b, i: (b * nq + i, 0))

    def kv(col, back):
        return pl.BlockSpec((tq, ATT_WIDTH), lambda b, i: (b * nq + jnp.maximum(i - back, 0), col))

    return pl.pallas_call(
        _att_kernel,
        grid=(batch, nq),
        in_specs=[qspec, kv(1, 2), kv(1, 1), kv(1, 0), kv(2, 2), kv(2, 1), kv(2, 0),
                  _resident((ATT_HEADS, tq, 3 * tq))],
        out_specs=pl.BlockSpec((tq, ATT_WIDTH), lambda b, i: (b * nq + i, 0)),
        out_shape=jax.ShapeDtypeStruct((m, ATT_WIDTH), BF16),
        compiler_params=pltpu.CompilerParams(
            dimension_semantics=("arbitrary", "arbitrary"),
            vmem_limit_bytes=_vmem_limit(40 * 1024 * 1024)),
        name="band_attention",
    )(att_qkv, att_qkv, att_qkv, att_qkv, att_qkv, att_qkv, att_qkv, bias_tab)


def _ret_kernel(q_ref, k_ref, v_ref, g_ref, dm_ref, qdec_ref, kdec_ref, gblk_ref, o_ref, state_ref):
    @pl.when(pl.program_id(1) == 0)
    def _():
        state_ref[...] = jnp.zeros_like(state_ref)

    q = q_ref[...]
    k = k_ref[...]
    kd = k.astype(F32) * kdec_ref[...]
    lane = lax.broadcasted_iota(jnp.int32, (1, LANES), 1)
    sub = lax.broadcasted_iota(jnp.int32, (LANES, 1), 0)
    for p in range(RET_QK_WIDTH // LANES):
        qp = q[:, p * LANES:(p + 1) * LANES]
        kp = k[:, p * LANES:(p + 1) * LANES]
        kd_t = kd[:, p * LANES:(p + 1) * LANES].T
        st = state_ref[p]
        st_b = st.astype(BF16)
        new_st = st * gblk_ref[p]
        for hh in range(2):
            h = 2 * p + hh
            qm = qp * ((lane // RET_QK_DIM) == hh).astype(BF16)
            s = lax.dot_general(qm, kp, (((1,), (1,)), ((), ())), preferred_element_type=F32)
            sd = (s * dm_ref[h]).astype(BF16)
            vh = v_ref[:, h * RET_V_DIM:(h + 1) * RET_V_DIM]
            inner = jnp.dot(sd, vh, preferred_element_type=F32)
            cross = jnp.dot(qm, st_b, preferred_element_type=F32) * qdec_ref[h]
            out = inner + cross
            mu = jnp.mean(out, -1, keepdims=True)
            oc = out - mu
            var = jnp.mean(oc * oc, -1, keepdims=True)
            nrm = oc * lax.rsqrt(var + LN_EPS)
            gate = g_ref[:, h * RET_V_DIM:(h + 1) * RET_V_DIM].astype(F32)
            o_ref[:, h * RET_V_DIM:(h + 1) * RET_V_DIM] = (nrm * gate).astype(BF16)
            kd_h = jnp.where((sub // RET_QK_DIM) == hh, kd_t, 0.0).astype(BF16)
            new_st = new_st + jnp.dot(kd_h, vh, preferred_element_type=F32)
        state_ref[p] = new_st


def _ret_tables():
    t = RET_BLOCK
    log_g = jnp.log(1.0 - jnp.power(2.0, -5.0 - jnp.arange(RET_HEADS, dtype=F32)))
    pos = jnp.arange(t, dtype=F32)
    n, mm = pos[:, None], pos[None, :]
    cn, cm = jnp.floor(n / CHUNK), jnp.floor(mm / CHUNK)
    expo = jnp.where(cn == cm, jnp.abs(n - mm), n - mm)
    dm = jnp.where((cm <= cn)[None], jnp.exp(log_g[:, None, None] * jnp.where(cm <= cn, expo, 0.0)[None]), 0.0)
    qdec = jnp.exp(log_g[:, None] * (pos + 1.0)[None, :])
    qdec = jnp.broadcast_to(qdec[:, :, None], (RET_HEADS, t, RET_V_DIM))
    kdec = jnp.exp(log_g[:, None] * (t - 1.0 - pos)[None, :])
    kdec = jnp.repeat(kdec.T, RET_QK_DIM, axis=1)
    gblk = jnp.repeat(jnp.exp(log_g * t), RET_QK_DIM).reshape(RET_QK_WIDTH // LANES, LANES, 1)
    gblk = jnp.broadcast_to(gblk, (RET_QK_WIDTH // LANES, LANES, RET_V_DIM))
    return dm, qdec, kdec, gblk


def _retention(ret_qk, ret_v, ret_g, tables, batch, seq):
    m = ret_qk.shape[0]
    t = RET_BLOCK
    assert seq % t == 0 and t % CHUNK == 0
    nb = seq // t
    dm, qdec, kdec, gblk = tables
    n_pairs = RET_QK_WIDTH // LANES
    rows = lambda w, col: pl.BlockSpec((t, w), lambda b, i: (b * nb + i, col))
    return pl.pallas_call(
        _ret_kernel,
        grid=(batch, nb),
        in_specs=[rows(RET_QK_WIDTH, 0), rows(RET_QK_WIDTH, 1), rows(RET_V_WIDTH, 0), rows(RET_V_WIDTH, 0),
                  _resident((RET_HEADS, t, t)), _resident((RET_HEADS, t, RET_V_DIM)),
                  _resident((t, RET_QK_WIDTH)), _resident((n_pairs, LANES, RET_V_DIM))],
        out_specs=rows(RET_V_WIDTH, 0),
        out_shape=jax.ShapeDtypeStruct((m, RET_V_WIDTH), BF16),
        scratch_shapes=[pltpu.VMEM((n_pairs, LANES, RET_V_DIM), F32)],
        compiler_params=pltpu.CompilerParams(
            dimension_semantics=("arbitrary", "arbitrary"),
            vmem_limit_bytes=_vmem_limit(40 * 1024 * 1024)),
        name="chunk_retention",
    )(ret_qk, ret_qk, ret_v, ret_g, dm, qdec, kdec, gblk)


def _s5_disc_kernel(lre_ref, lim_ref, lstep_ref, bre_ref, bim_ref, lbr_ref, lbi_ref, bbr_ref, bbi_ref):
    lre, lim = lre_ref[...], lim_ref[...]
    step = jnp.exp(lstep_ref[...])
    mag = jnp.exp(lre * step)
    ang = lim * step
    lbr = mag * jnp.cos(ang)
    lbi = mag * jnp.sin(ang)
    den = lre * lre + lim * lim
    nr = lbr - 1.0
    cr = (nr * lre + lbi * lim) / den
    ci = (lbi * lre - nr * lim) / den
    lbr_ref[...] = lbr
    lbi_ref[...] = lbi
    bre, bim = bre_ref[...], bim_ref[...]
    bbr_ref[...] = cr * bre - ci * bim
    bbi_ref[...] = cr * bim + ci * bre


def _s5_discretise(lam_re, lam_im, log_step, b_re, b_im):
    gp = SSM_GROUPS * SSM_STATE
    row = lambda a: a.reshape(1, gp).astype(F32)
    lstep = jnp.repeat(log_step.astype(F32), SSM_STATE).reshape(1, gp)
    b_t = lambda b: b.astype(F32).reshape(gp, SSM_GROUP).T
    full = lambda s: pl.BlockSpec(s, lambda: (0,) * len(s))
    return pl.pallas_call(
        _s5_disc_kernel,
        in_specs=[full((1, gp))] * 3 + [full((SSM_GROUP, gp))] * 2,
        out_specs=[full((1, gp))] * 2 + [full((SSM_GROUP, gp))] * 2,
        out_shape=[jax.ShapeDtypeStruct((1, gp), F32)] * 2 + [jax.ShapeDtypeStruct((SSM_GROUP, gp), F32)] * 2,
        name="s5_discretise",
    )(row(lam_re), row(lam_im), lstep, b_t(b_re), b_t(b_im))


def _s5_matrices(lbr, lbi, bbr, bbi, c_re, c_im, batch):
    nt, gl = SSM_LANE_TILES, LANES // SSM_GROUP
    eye = jnp.eye(gl, dtype=F32)

    def expand(bb):
        bb = bb.reshape(SSM_GROUP, nt, gl, SSM_STATE).transpose(1, 2, 0, 3)
        return jnp.einsum('tgip,gh->tgihp', bb, eye).reshape(nt, LANES, SSM_TILE_STATE)

    def contract(c):
        c = c.astype(F32).reshape(nt, gl, SSM_GROUP, SSM_STATE)
        return jnp.einsum('tgop,gh->tgpho', c, eye).reshape(nt, SSM_TILE_STATE, LANES)

    b_big = jnp.concatenate([expand(bbr), expand(bbi)], axis=-1).astype(BF16)
    c_big = jnp.concatenate([contract(c_re), -contract(c_im)], axis=1).astype(BF16)
    lam_r = jnp.broadcast_to(lbr, (batch, SSM_GROUPS * SSM_STATE))
    lam_i = jnp.broadcast_to(lbi, (batch, SSM_GROUPS * SSM_STATE))
    return b_big, c_big, lam_r, lam_i


def _s5_kernel(batch, u_ref, bbig_ref, cbig_ref, lr_ref, li_ref, dskip_ref, wglu_ref, y_ref,
               x_ref, st_ref, tb_ref):
    @pl.when(pl.program_id(0) == 0)
    def _():
        st_ref[...] = jnp.zeros_like(st_ref)

    ts = SSM_TILE_STATE
    for b in range(batch):
        for kt in range(SSM_LANE_TILES):
            c0 = b * SSM_WIDTH + kt * LANES
            tb_ref[kt, pl.ds(b, SSM_T_BLOCK, stride=batch), :] = u_ref[:, c0:c0 + LANES].astype(F32)
    u_f32 = [tb_ref[kt] for kt in range(SSM_LANE_TILES)]
    for kt in range(SSM_LANE_TILES):
        x_ref[:, 2 * ts * kt:2 * ts * (kt + 1)] = jnp.dot(
            u_f32[kt].astype(BF16), bbig_ref[kt], preferred_element_type=F32)

    for kt in range(SSM_LANE_TILES):
        re0, im0 = 2 * ts * kt, 2 * ts * kt + ts
        lr = lr_ref[:, kt * ts:(kt + 1) * ts]
        li = li_ref[:, kt * ts:(kt + 1) * ts]

        def step(t, carry, re0=re0, im0=im0, lr=lr, li=li):
            xr, xi = carry
            rows = pl.ds(pl.multiple_of(t * batch, batch), batch)
            nr = lr * xr - li * xi + x_ref[rows, re0:re0 + ts]
            ni = lr * xi + li * xr + x_ref[rows, im0:im0 + ts]
            x_ref[rows, re0:re0 + ts] = nr
            x_ref[rows, im0:im0 + ts] = ni
            return nr, ni

        xr, xi = lax.fori_loop(0, SSM_T_BLOCK, step,
                               (st_ref[:, re0:re0 + ts], st_ref[:, im0:im0 + ts]), unroll=4)
        st_ref[:, re0:re0 + ts] = xr
        st_ref[:, im0:im0 + ts] = xi

    ys = []
    for kt in range(SSM_LANE_TILES):
        xs = x_ref[:, 2 * ts * kt:2 * ts * (kt + 1)].astype(BF16)
        ys.append(jnp.dot(xs, cbig_ref[kt], preferred_element_type=F32))
    y = jnp.concatenate(ys, axis=-1) + dskip_ref[...] * jnp.concatenate(u_f32, axis=-1)
    y = _gelu(y)
    y = y * jax.nn.sigmoid(jnp.dot(y.astype(BF16), wglu_ref[...], preferred_element_type=F32))
    for kt in range(SSM_LANE_TILES):
        tb_ref[kt] = y[:, kt * LANES:(kt + 1) * LANES]
    for b in range(batch):
        for kt in range(SSM_LANE_TILES):
            c0 = b * SSM_WIDTH + kt * LANES
            y_ref[:, c0:c0 + LANES] = tb_ref[kt, pl.ds(b, SSM_T_BLOCK, stride=batch), :].astype(BF16)


def _s5(u_sb, mats, d_skip, w_glu, batch, seq):
    b_big, c_big, lam_r, lam_i = mats
    rows = SSM_T_BLOCK * batch
    assert seq % SSM_T_BLOCK == 0 and batch % 8 == 0
    n_state = 2 * SSM_TILE_STATE * SSM_LANE_TILES
    blk = pl.BlockSpec((SSM_T_BLOCK, batch * SSM_WIDTH), lambda j: (j, 0))
    return pl.pallas_call(
        functools.partial(_s5_kernel, batch),
        grid=(seq // SSM_T_BLOCK,),
        in_specs=[blk, _resident(b_big.shape), _resident(c_big.shape), _resident(lam_r.shape),
                  _resident(lam_i.shape), _resident((1, SSM_WIDTH)), _resident((SSM_WIDTH, SSM_WIDTH))],
        out_specs=blk,
        out_shape=jax.ShapeDtypeStruct((seq, batch * SSM_WIDTH), BF16),
        scratch_shapes=[pltpu.VMEM((rows, n_state), F32), pltpu.VMEM((batch, n_state), F32),
                        pltpu.VMEM((SSM_LANE_TILES, rows, LANES), F32)],
        compiler_params=pltpu.CompilerParams(
            dimension_semantics=("arbitrary",), vmem_limit_bytes=_vmem_limit(40 * 1024 * 1024)),
        name="s5_scan",
    )(u_sb, b_big, c_big, lam_r, lam_i, d_skip, w_glu)


def _merge_kernel(att_ref, ret_ref, yc_ref, gate_ref, h_ref, wa_ref, wb_ref, wc_ref, wo_ref,
                  g_ref, b_ref, out_ref):
    pa = jnp.dot(att_ref[...], wa_ref[...], preferred_element_type=F32)
    pb = jnp.dot(ret_ref[...], wb_ref[...], preferred_element_type=F32)
    pc = jnp.dot(yc_ref[...], wc_ref[...], preferred_element_type=F32)
    d = D_MODEL
    merged = (gate_ref[:, 0:d].astype(F32) * pa + gate_ref[:, d:2 * d].astype(F32) * pb
              + gate_ref[:, 2 * d:3 * d].astype(F32) * pc)
    y = DEEPNORM_ALPHA * h_ref[...] + jnp.dot(merged.astype(BF16), wo_ref[...], preferred_element_type=F32)
    out_ref[...] = _layer_norm(y, g_ref[...], b_ref[...])


def _merge(att_o, ret_o, yc_sb, gates, h, wa, wb, wc, wo, ln_g, ln_b, batch, seq):
    m = h.shape[0]
    tm = ROW_TILE
    n_s = seq // tm
    row = lambda w: pl.BlockSpec((tm, w), lambda i: (i, 0))
    return pl.pallas_call(
        _merge_kernel,
        grid=(m // tm,),
        in_specs=[row(ATT_WIDTH), row(RET_V_WIDTH),
                  pl.BlockSpec((tm, SSM_WIDTH), lambda i: (i % n_s, i // n_s)),
                  row(N_BRANCH * D_MODEL), row(D_MODEL),
                  _resident(wa.shape), _resident(wb.shape), _resident(wc.shape), _resident(wo.shape),
                  _resident((1, D_MODEL)), _resident((1, D_MODEL))],
        out_specs=row(D_MODEL),
        out_shape=jax.ShapeDtypeStruct((m, D_MODEL), F32),
        compiler_params=pltpu.CompilerParams(
            dimension_semantics=("arbitrary",), vmem_limit_bytes=_vmem_limit(40 * 1024 * 1024)),
        name="gated_merge",
    )(att_o, ret_o, yc_sb, gates, h, wa, wb, wc, wo, ln_g, ln_b)


FF_HALO = 8


def _ffn_kernel(n_s, h_ref, wup_ref, cw_ref, cb_ref, wdn_ref, g_ref, b_ref, out_ref, ext_ref, acc_ref):
    tm = ROW_TILE
    i = pl.program_id(0)

    @pl.when(i % n_s == 0)
    def _():
        ext_ref[0:FF_HALO, :] = jnp.zeros((FF_HALO, 2 * D_FF), F32)

    @pl.when(i % n_s != 0)
    def _():
        ext_ref[0:FF_HALO, :] = ext_ref[tm:tm + FF_HALO, :]

    h = h_ref[...]
    hb = h.astype(BF16)
    for c in range(2 * D_FF // COL_CHUNK):
        cols = slice(c * COL_CHUNK, (c + 1) * COL_CHUNK)
        ext_ref[FF_HALO:FF_HALO + tm, cols] = jnp.dot(hb, wup_ref[:, cols], preferred_element_type=F32)

    def conv(col):
        cols = slice(col, col + FF_CHUNK)
        w = cw_ref[:, cols]
        return (ext_ref[FF_HALO - 2:FF_HALO - 2 + tm, cols] * w[0:1]
                + ext_ref[FF_HALO - 1:FF_HALO - 1 + tm, cols] * w[1:2]
                + ext_ref[FF_HALO:FF_HALO + tm, cols] * w[2:3] + cb_ref[:, cols])

    for j in range(D_FF // FF_CHUNK):
        act = (_gelu(conv(j * FF_CHUNK)) * conv(D_FF + j * FF_CHUNK)).astype(BF16)
        part = jnp.dot(act, wdn_ref[j * FF_CHUNK:(j + 1) * FF_CHUNK, :], preferred_element_type=F32)
        if j == 0:
            acc_ref[...] = part
        else:
            acc_ref[...] += part
    out_ref[...] = _layer_norm(DEEPNORM_ALPHA * h + acc_ref[...], g_ref[...], b_ref[...])


def _ffn(h, w_up, conv_w, conv_b, w_down, ln_g, ln_b, seq):
    m = h.shape[0]
    tm = ROW_TILE
    assert (2 * D_FF) % COL_CHUNK == 0 and D_FF % FF_CHUNK == 0
    row = pl.BlockSpec((tm, D_MODEL), lambda i: (i, 0))
    return pl.pallas_call(
        functools.partial(_ffn_kernel, seq // tm),
        grid=(m // tm,),
        in_specs=[row, _resident(w_up.shape), _resident(conv_w.shape), _resident((1, 2 * D_FF)),
                  _resident(w_down.shape), _resident((1, D_MODEL)), _resident((1, D_MODEL))],
        out_specs=row,
        out_shape=jax.ShapeDtypeStruct((m, D_MODEL), F32),
        scratch_shapes=[pltpu.VMEM((tm + 2 * FF_HALO, 2 * D_FF), F32), pltpu.VMEM((tm, D_MODEL), F32)],
        compiler_params=pltpu.CompilerParams(
            dimension_semantics=("arbitrary",), vmem_limit_bytes=_vmem_limit(56 * 1024 * 1024)),
        name="conv_ffn",
    )(h, w_up, conv_w, conv_b, w_down, ln_g, ln_b)


def _rotary_tables(seq):
    half = RET_QK_DIM // 2
    inv = ROPE_BASE ** (-jnp.arange(0, RET_QK_DIM, 2, dtype=F32) / RET_QK_DIM)
    ang = jnp.arange(seq, dtype=F32)[:, None] * inv[None, :]
    cos, sin = jnp.cos(ang), jnp.sin(ang)
    assert cos.shape[1] == half
    cos_t = jnp.tile(jnp.concatenate([cos, cos], -1), (1, RET_HEADS))
    sin_t = jnp.tile(jnp.concatenate([-sin, sin], -1), (1, RET_HEADS))
    return cos_t, sin_t


def kernel(x, ln_in_g, ln_in_b, w_in, rel_bias, w_proj_a, w_proj_b, w_proj_c, lam_re, lam_im, log_step,
           b_re, b_im, c_re, c_im, d_skip, w_glu, w_o, ln1_g, ln1_b, w_up, conv_w, conv_b, w_down,
           ln2_g, ln2_b):
    batch, seq, d = x.shape
    assert d == D_MODEL and seq % ROW_TILE == 0 and w_in.shape[0] == DEPTH
    m = batch * seq
    vec = lambda a: a.reshape(1, -1).astype(F32)
    cos_t, sin_t = _rotary_tables(seq)
    ret_tabs = _ret_tables()
    h = x.reshape(m, d)
    for l in range(DEPTH):
        w_l = w_in[l].astype(BF16)
        if l == 0:
            h, att_qkv, ret_qk, ret_v, ret_g, u_sb, gates = _in_proj(
                h, vec(ln_in_g), vec(ln_in_b), w_l, cos_t, sin_t, batch, seq)
        else:
            att_qkv, ret_qk, ret_v, ret_g, u_sb, gates = _in_proj(
                h, None, None, w_l, cos_t, sin_t, batch, seq)
        att_o = _attention(att_qkv, _att_bias_table(rel_bias[l]), batch, seq)
        ret_o = _retention(ret_qk, ret_v, ret_g, ret_tabs, batch, seq)
        lbr, lbi, bbr, bbi = _s5_discretise(lam_re[l], lam_im[l], log_step[l], b_re[l], b_im[l])
        mats = _s5_matrices(lbr, lbi, bbr, bbi, c_re[l], c_im[l], batch)
        yc_sb = _s5(u_sb, mats, vec(d_skip[l]), w_glu[l].astype(BF16), batch, seq)
        h = _merge(att_o, ret_o, yc_sb, gates, h, w_proj_a[l].astype(BF16), w_proj_b[l].astype(BF16),
                   w_proj_c[l].astype(BF16), w_o[l].astype(BF16), vec(ln1_g[l]), vec(ln1_b[l]), batch, seq)
        h = _ffn(h, w_up[l].astype(BF16), conv_w[l].astype(F32), vec(conv_b[l]), w_down[l].astype(BF16),
                 vec(ln2_g[l]), vec(ln2_b[l]), seq)
    return h.reshape(batch, seq, d)
```

```python
import functools
import math

import jax
import jax.numpy as jnp
import numpy as np
from jax import lax
from jax.experimental import pallas as pl
from jax.experimental.pallas import tpu as pltpu

F32 = jnp.float32
BF16 = jnp.bfloat16

D_MODEL = 1024
DEPTH = 2
CHUNK = 64
ATT_HEADS = 8
ATT_HEAD_DIM = 64
ATT_WIDTH = ATT_HEADS * ATT_HEAD_DIM
LEFT_CHUNKS = 8
MAX_REL = 128
RET_HEADS = 8
RET_QK_DIM = 64
RET_V_DIM = 128
RET_QK_WIDTH = RET_HEADS * RET_QK_DIM
RET_V_WIDTH = RET_HEADS * RET_V_DIM
ROPE_BASE = 10000.0
SSM_WIDTH = 512
SSM_GROUP = 16
SSM_GROUPS = SSM_WIDTH // SSM_GROUP
SSM_STATE = 64
D_FF = 2816
N_BRANCH = 3
IN_WIDTH = 8192
DEEPNORM_ALPHA = (2.0 * DEPTH) ** 0.25
LN_EPS = 1e-5
MASK_VALUE = -1e30
LOG2E = math.log2(math.e)

LANES = 128
V7X_VMEM_BYTES = 64 * 1024 * 1024

ROW_TILE = 512
COL_CHUNK = 512
ATT_Q_BLOCK = 256
RET_BLOCK = 256
RET_BATCH_PER_STEP = 4
SSM_T_BLOCK = 64
SSM_LANE_TILES = SSM_WIDTH // LANES
SSM_TILE_STATE = (LANES // SSM_GROUP) * SSM_STATE
FF_CHUNK = 256


def _vmem_limit(nbytes):
    return int(min(nbytes, V7X_VMEM_BYTES - 6 * 1024 * 1024))


def _resident(shape):
    nd = len(shape)
    return pl.BlockSpec(shape, lambda *_: (0,) * nd, pipeline_mode=pl.Buffered(1))


def _layer_norm(x, g, b):
    mu = jnp.mean(x, -1, keepdims=True)
    xc = x - mu
    var = jnp.mean(xc * xc, -1, keepdims=True)
    return xc * lax.rsqrt(var + LN_EPS) * g + b


def _gelu(x):
    return 0.5 * x * (1.0 + jnp.tanh(math.sqrt(2.0 / math.pi) * (x + 0.044715 * (x * x * x))))


def _in_proj_kernel(apply_ln, *refs):
    if apply_ln:
        (x_ref, g_ref, b_ref, w_ref, cos_ref, sin_ref,
         hf_ref, att_ref, rqk_ref, rv_ref, rg_ref, u_ref, gate_ref) = refs
    else:
        (x_ref, w_ref, cos_ref, sin_ref,
         att_ref, rqk_ref, rv_ref, rg_ref, u_ref, gate_ref) = refs
    x = x_ref[...]
    if apply_ln:
        x = _layer_norm(x, g_ref[...], b_ref[...])
        hf_ref[...] = x
    xb = x.astype(BF16)

    def proj(c):
        return jnp.dot(xb, w_ref[:, c * COL_CHUNK:(c + 1) * COL_CHUNK], preferred_element_type=F32)

    lane = lax.broadcasted_iota(jnp.int32, (1, COL_CHUNK), 1)
    first_half = (lane % RET_QK_DIM) < (RET_QK_DIM // 2)
    half = RET_QK_DIM // 2

    def rotary(a):
        swapped = jnp.where(first_half, pltpu.roll(a, COL_CHUNK - half, 1), pltpu.roll(a, half, 1))
        return a * cos_ref[...] + swapped * sin_ref[...]

    att_ref[:, 0:512] = (proj(0) * (ATT_HEAD_DIM ** -0.5 * LOG2E)).astype(BF16)
    att_ref[:, 512:1024] = proj(1).astype(BF16)
    att_ref[:, 1024:1536] = proj(2).astype(BF16)
    rqk_ref[:, 0:512] = (rotary(proj(3)) * (RET_QK_DIM ** -0.5)).astype(BF16)
    rqk_ref[:, 512:1024] = rotary(proj(4)).astype(BF16)
    rv_ref[:, 0:512] = proj(5).astype(BF16)
    rv_ref[:, 512:1024] = proj(6).astype(BF16)
    for c in range(2):
        a = proj(7 + c)
        rg_ref[:, c * 512:(c + 1) * 512] = (a * jax.nn.sigmoid(a)).astype(BF16)
    u_ref[...] = proj(9).astype(BF16)
    for c in range(6):
        gate_ref[:, c * 512:(c + 1) * 512] = jax.nn.sigmoid(proj(10 + c)).astype(BF16)


def _in_proj(x2d, ln_g, ln_b, w_bf16, cos_t, sin_t, batch, seq):
    m = x2d.shape[0]
    tm = ROW_TILE
    n_s = seq // tm
    apply_ln = ln_g is not None
    row = lambda w: pl.BlockSpec((tm, w), lambda i: (i, 0))
    in_specs = [row(D_MODEL)]
    args = [x2d]
    if apply_ln:
        in_specs += [_resident((1, D_MODEL)), _resident((1, D_MODEL))]
        args += [ln_g, ln_b]
    in_specs += [_resident((D_MODEL, IN_WIDTH)),
                 pl.BlockSpec((tm, COL_CHUNK), lambda i: (i % n_s, 0)),
                 pl.BlockSpec((tm, COL_CHUNK), lambda i: (i % n_s, 0))]
    args += [w_bf16, cos_t, sin_t]
    out_shape, out_specs = [], []
    if apply_ln:
        out_shape.append(jax.ShapeDtypeStruct((m, D_MODEL), F32))
        out_specs.append(row(D_MODEL))
    for w in (3 * ATT_WIDTH, 2 * RET_QK_WIDTH, RET_V_WIDTH, RET_V_WIDTH):
        out_shape.append(jax.ShapeDtypeStruct((m, w), BF16))
        out_specs.append(row(w))
    out_shape.append(jax.ShapeDtypeStruct((seq, batch * SSM_WIDTH), BF16))
    out_specs.append(pl.BlockSpec((tm, SSM_WIDTH), lambda i: (i % n_s, i // n_s)))
    out_shape.append(jax.ShapeDtypeStruct((m, N_BRANCH * D_MODEL), BF16))
    out_specs.append(row(N_BRANCH * D_MODEL))
    return pl.pallas_call(
        functools.partial(_in_proj_kernel, apply_ln),
        grid=(m // tm,),
        in_specs=in_specs,
        out_specs=out_specs,
        out_shape=out_shape,
        compiler_params=pltpu.CompilerParams(
            dimension_semantics=("arbitrary",), vmem_limit_bytes=_vmem_limit(56 * 1024 * 1024)),
        name="in_proj_ln" if apply_ln else "in_proj",
    )(*args)


ATT_Q_PER_STEP = 2


def _att_kernel(q_ref, *refs):
    n_kv = ATT_Q_PER_STEP + 2
    k_refs, v_refs, (bias_ref, o_ref) = refs[:n_kv], refs[n_kv:2 * n_kv], refs[2 * n_kv:]
    tq = ATT_Q_BLOCK
    k_all = jnp.concatenate([r[...] for r in k_refs], axis=0)
    v_all = jnp.concatenate([r[...] for r in v_refs], axis=0)
    lane = lax.broadcasted_iota(jnp.int32, (1, LANES), 1)
    first = (lane == 0).astype(BF16)
    row = lax.broadcasted_iota(jnp.int32, (3 * tq, 1), 0)
    q_one = jnp.broadcast_to(first, (tq, LANES))
    v_ones = jnp.ones((3 * tq, LANES), BF16)
    for sub in range(ATT_Q_PER_STEP):
        qb = pl.program_id(1) * ATT_Q_PER_STEP + sub
        q = q_ref[sub * tq:(sub + 1) * tq, :]
        k = k_all[sub * tq:(sub + 3) * tq, :]
        v = v_all[sub * tq:(sub + 3) * tq, :]
        pen = jnp.where(row < tq, jnp.where(qb >= 2, 0.0, MASK_VALUE),
                        jnp.where(row < 2 * tq, jnp.where(qb >= 1, 0.0, MASK_VALUE), 0.0))
        k_pen = pen.astype(BF16) * first
        for p in range(ATT_WIDTH // LANES):
            qp = q[:, p * LANES:(p + 1) * LANES]
            k_ext = jnp.concatenate([k[:, p * LANES:(p + 1) * LANES], k_pen], axis=1)
            v_ext = jnp.concatenate([v[:, p * LANES:(p + 1) * LANES], v_ones], axis=1)
            outs = []
            for hh in range(2):
                h = 2 * p + hh
                head_lanes = ((lane // ATT_HEAD_DIM) == hh).astype(BF16)
                q_ext = jnp.concatenate([qp * head_lanes, q_one], axis=1)
                s = lax.dot_general(q_ext, k_ext, (((1,), (1,)), ((), ())), preferred_element_type=F32)
                s = s + bias_ref[h]
                e = jnp.exp2(s - jnp.max(s, -1, keepdims=True))
                o = jnp.dot(e.astype(BF16), v_ext, preferred_element_type=F32)
                outs.append(o[:, :LANES] / o[:, LANES:])
            o_ref[sub * tq:(sub + 1) * tq, p * LANES:(p + 1) * LANES] = jnp.where(
                (lane // ATT_HEAD_DIM) == 0, outs[0], outs[1]).astype(BF16)


def _att_band_mask():
    tq = ATT_Q_BLOCK
    qc = (np.arange(tq) // CHUNK)[:, None]
    kc = (np.arange(3 * tq) // CHUNK)[None, :]
    return (kc >= qc) & (kc <= qc + LEFT_CHUNKS)


def _att_bias_table(rel_bias):
    tq = ATT_Q_BLOCK
    period = 4 * tq
    rb = rel_bias.astype(F32) * LOG2E
    nh = rb.shape[0]
    fill = lambda col, n: jnp.broadcast_to(rb[:, col:col + 1], (nh, n))
    g = jnp.concatenate([fill(2 * MAX_REL, 2 * tq - MAX_REL), rb[:, ::-1],
                         fill(0, tq - MAX_REL - 1), fill(2 * MAX_REL, tq)], axis=1)
    assert g.shape[1] == period
    toep = jnp.tile(g, (1, tq))[:, :tq * (period - 1)].reshape(nh, tq, period - 1)[:, :, :3 * tq]
    return jnp.where(_att_band_mask()[None], toep, MASK_VALUE)


def _attention(att_qkv, bias_tab, batch, seq):
    m = att_qkv.shape[0]
    tq = ATT_Q_BLOCK
    n_sub = ATT_Q_PER_STEP
    assert LEFT_CHUNKS * CHUNK == 2 * tq and seq % (n_sub * tq) == 0
    nq = seq // tq
    n_steps = nq // n_sub
    qspec = pl.BlockSpec((n_sub * tq, ATT_WIDTH), lambda b, i: (b * n_steps + i, 0))

    def kv(col, off):
        return pl.BlockSpec((tq, ATT_WIDTH),
                            lambda b, i: (b * nq + jnp.maximum(i * n_sub - 2 + off, 0), col))

    n_kv = n_sub + 2
    return pl.pallas_call(
        _att_kernel,
        grid=(batch, n_steps),
        in_specs=([qspec] + [kv(1, j) for j in range(n_kv)] + [kv(2, j) for j in range(n_kv)]
                  + [_resident((ATT_HEADS, tq, 3 * tq))]),
        out_specs=pl.BlockSpec((n_sub * tq, ATT_WIDTH), lambda b, i: (b * n_steps + i, 0)),
        out_shape=jax.ShapeDtypeStruct((m, ATT_WIDTH), BF16),
        compiler_params=pltpu.CompilerParams(
            dimension_semantics=("arbitrary", "arbitrary"),
            vmem_limit_bytes=_vmem_limit(40 * 1024 * 1024)),
        name="band_attention",
    )(*([att_qkv] * (1 + 2 * n_kv)), bias_tab)


def _ret_kernel(q_ref, k_ref, v_ref, g_ref, dm_ref, qdec_ref, kdec_ref, gblk_ref, o_ref, state_ref):
    @pl.when(pl.program_id(1) == 0)
    def _():
        state_ref[...] = jnp.zeros_like(state_ref)

    lane = lax.broadcasted_iota(jnp.int32, (1, LANES), 1)
    sub = lax.broadcasted_iota(jnp.int32, (LANES, 1), 0)
    for sb in range(RET_BATCH_PER_STEP):
        q = q_ref[sb]
        k = k_ref[sb]
        kd = k.astype(F32) * kdec_ref[...]
        for p in range(RET_QK_WIDTH // LANES):
            qp = q[:, p * LANES:(p + 1) * LANES]
            kp = k[:, p * LANES:(p + 1) * LANES]
            kd_t = kd[:, p * LANES:(p + 1) * LANES].T
            st = state_ref[sb, p]
            st_b = st.astype(BF16)
            new_st = st * gblk_ref[p]
            for hh in range(2):
                h = 2 * p + hh
                qm = qp * ((lane // RET_QK_DIM) == hh).astype(BF16)
                s = lax.dot_general(qm, kp, (((1,), (1,)), ((), ())), preferred_element_type=F32)
                sd = (s * dm_ref[h]).astype(BF16)
                vh = v_ref[sb, :, h * RET_V_DIM:(h + 1) * RET_V_DIM]
                inner = jnp.dot(sd, vh, preferred_element_type=F32)
                cross = jnp.dot(qm, st_b, preferred_element_type=F32) * qdec_ref[h]
                out = inner + cross
                mu = jnp.mean(out, -1, keepdims=True)
                oc = out - mu
                var = jnp.mean(oc * oc, -1, keepdims=True)
                nrm = oc * lax.rsqrt(var + LN_EPS)
                gate = g_ref[sb, :, h * RET_V_DIM:(h + 1) * RET_V_DIM].astype(F32)
                o_ref[sb, :, h * RET_V_DIM:(h + 1) * RET_V_DIM] = (nrm * gate).astype(BF16)
                kd_h = jnp.where((sub // RET_QK_DIM) == hh, kd_t, 0.0).astype(BF16)
                new_st = new_st + jnp.dot(kd_h, vh, preferred_element_type=F32)
            state_ref[sb, p] = new_st


def _ret_tables():
    t = RET_BLOCK
    log_g = jnp.log(1.0 - jnp.power(2.0, -5.0 - jnp.arange(RET_HEADS, dtype=F32)))
    pos = jnp.arange(t, dtype=F32)
    n, mm = pos[:, None], pos[None, :]
    cn, cm = jnp.floor(n / CHUNK), jnp.floor(mm / CHUNK)
    expo = jnp.where(cn == cm, jnp.abs(n - mm), n - mm)
    dm = jnp.where((cm <= cn)[None], jnp.exp(log_g[:, None, None] * jnp.where(cm <= cn, expo, 0.0)[None]), 0.0)
    qdec = jnp.exp(log_g[:, None] * (pos + 1.0)[None, :])
    qdec = jnp.broadcast_to(qdec[:, :, None], (RET_HEADS, t, RET_V_DIM))
    kdec = jnp.exp(log_g[:, None] * (t - 1.0 - pos)[None, :])
    kdec = jnp.repeat(kdec.T, RET_QK_DIM, axis=1)
    gblk = jnp.repeat(jnp.exp(log_g * t), RET_QK_DIM).reshape(RET_QK_WIDTH // LANES, LANES, 1)
    gblk = jnp.broadcast_to(gblk, (RET_QK_WIDTH // LANES, LANES, RET_V_DIM))
    return dm, qdec, kdec, gblk


def _retention(ret_qk, ret_v, ret_g, tables, batch, seq):
    m = ret_qk.shape[0]
    t = RET_BLOCK
    nsb = RET_BATCH_PER_STEP
    assert seq % t == 0 and t % CHUNK == 0 and batch % nsb == 0
    dm, qdec, kdec, gblk = tables
    n_pairs = RET_QK_WIDTH // LANES
    rows = lambda w, col: pl.BlockSpec((nsb, t, w), lambda b, i: (b, i, col))
    as3d = lambda a: a.reshape(batch, seq, a.shape[-1])
    out = pl.pallas_call(
        _ret_kernel,
        grid=(batch // nsb, seq // t),
        in_specs=[rows(RET_QK_WIDTH, 0), rows(RET_QK_WIDTH, 1), rows(RET_V_WIDTH, 0), rows(RET_V_WIDTH, 0),
                  _resident((RET_HEADS, t, t)), _resident((RET_HEADS, t, RET_V_DIM)),
                  _resident((t, RET_QK_WIDTH)), _resident((n_pairs, LANES, RET_V_DIM))],
        out_specs=rows(RET_V_WIDTH, 0),
        out_shape=jax.ShapeDtypeStruct((batch, seq, RET_V_WIDTH), BF16),
        scratch_shapes=[pltpu.VMEM((nsb, n_pairs, LANES, RET_V_DIM), F32)],
        compiler_params=pltpu.CompilerParams(
            dimension_semantics=("arbitrary", "arbitrary"),
            vmem_limit_bytes=_vmem_limit(40 * 1024 * 1024)),
        name="chunk_retention",
    )(as3d(ret_qk), as3d(ret_qk), as3d(ret_v), as3d(ret_g), dm, qdec, kdec, gblk)
    return out.reshape(m, RET_V_WIDTH)


def _s5_disc_kernel(lre_ref, lim_ref, lstep_ref, bre_ref, bim_ref, lbr_ref, lbi_ref, bbr_ref, bbi_ref):
    lre, lim = lre_ref[...], lim_ref[...]
    step = jnp.exp(lstep_ref[...])
    mag = jnp.exp(lre * step)
    ang = lim * step
    lbr = mag * jnp.cos(ang)
    lbi = mag * jnp.sin(ang)
    den = lre * lre + lim * lim
    nr = lbr - 1.0
    cr = (nr * lre + lbi * lim) / den
    ci = (lbi * lre - nr * lim) / den
    lbr_ref[...] = lbr
    lbi_ref[...] = lbi
    bre, bim = bre_ref[...], bim_ref[...]
    bbr_ref[...] = cr * bre - ci * bim
    bbi_ref[...] = cr * bim + ci * bre


def _s5_discretise(lam_re, lam_im, log_step, b_re, b_im):
    gp = SSM_GROUPS * SSM_STATE
    row = lambda a: a.reshape(1, gp).astype(F32)
    lstep = jnp.repeat(log_step.astype(F32), SSM_STATE).reshape(1, gp)
    b_t = lambda b: b.astype(F32).reshape(gp, SSM_GROUP).T
    full = lambda s: pl.BlockSpec(s, lambda: (0,) * len(s))
    return pl.pallas_call(
        _s5_disc_kernel,
        in_specs=[full((1, gp))] * 3 + [full((SSM_GROUP, gp))] * 2,
        out_specs=[full((1, gp))] * 2 + [full((SSM_GROUP, gp))] * 2,
        out_shape=[jax.ShapeDtypeStruct((1, gp), F32)] * 2 + [jax.ShapeDtypeStruct((SSM_GROUP, gp), F32)] * 2,
        name="s5_discretise",
    )(row(lam_re), row(lam_im), lstep, b_t(b_re), b_t(b_im))


def _s5_matrices(lbr, lbi, bbr, bbi, c_re, c_im, batch):
    nt, gl = SSM_LANE_TILES, LANES // SSM_GROUP
    eye = jnp.eye(gl, dtype=F32)

    def expand(bb):
        bb = bb.reshape(SSM_GROUP, nt, gl, SSM_STATE).transpose(1, 2, 0, 3)
        return jnp.einsum('tgip,gh->tgihp', bb, eye).reshape(nt, LANES, SSM_TILE_STATE)

    def contract(c):
        c = c.astype(F32).reshape(nt, gl, SSM_GROUP, SSM_STATE)
        return jnp.einsum('tgop,gh->tgpho', c, eye).reshape(nt, SSM_TILE_STATE, LANES)

    b_big = jnp.concatenate([expand(bbr), expand(bbi)], axis=-1).astype(BF16)
    c_big = jnp.concatenate([contract(c_re), -contract(c_im)], axis=1).astype(BF16)
    lam_r = jnp.broadcast_to(lbr, (batch, SSM_GROUPS * SSM_STATE))
    lam_i = jnp.broadcast_to(lbi, (batch, SSM_GROUPS * SSM_STATE))
    return b_big, c_big, lam_r, lam_i


def _s5_kernel(batch, u_ref, bbig_ref, cbig_ref, lr_ref, li_ref, dskip_ref, wglu_ref, y_ref,
               x_ref, st_ref, tb_ref):
    @pl.when(pl.program_id(0) == 0)
    def _():
        st_ref[...] = jnp.zeros_like(st_ref)

    ts = SSM_TILE_STATE
    for b in range(batch):
        for kt in range(SSM_LANE_TILES):
            c0 = b * SSM_WIDTH + kt * LANES
            tb_ref[kt, pl.ds(b, SSM_T_BLOCK, stride=batch), :] = u_ref[:, c0:c0 + LANES].astype(F32)
    u_f32 = [tb_ref[kt] for kt in range(SSM_LANE_TILES)]
    for kt in range(SSM_LANE_TILES):
        x_ref[:, 2 * ts * kt:2 * ts * (kt + 1)] = jnp.dot(
            u_f32[kt].astype(BF16), bbig_ref[kt], preferred_element_type=F32)

    for kt in range(SSM_LANE_TILES):
        re0, im0 = 2 * ts * kt, 2 * ts * kt + ts
        lr = lr_ref[:, kt * ts:(kt + 1) * ts]
        li = li_ref[:, kt * ts:(kt + 1) * ts]

        xr, xi = st_ref[:, re0:re0 + ts], st_ref[:, im0:im0 + ts]
        for t in range(SSM_T_BLOCK):
            rows = slice(t * batch, (t + 1) * batch)
            xr, xi = (lr * xr - li * xi + x_ref[rows, re0:re0 + ts],
                      lr * xi + li * xr + x_ref[rows, im0:im0 + ts])
            x_ref[rows, re0:re0 + ts] = xr
            x_ref[rows, im0:im0 + ts] = xi
        st_ref[:, re0:re0 + ts] = xr
        st_ref[:, im0:im0 + ts] = xi

    ys = []
    for kt in range(SSM_LANE_TILES):
        xs = x_ref[:, 2 * ts * kt:2 * ts * (kt + 1)].astype(BF16)
        ys.append(jnp.dot(xs, cbig_ref[kt], preferred_element_type=F32))
    y = jnp.concatenate(ys, axis=-1) + dskip_ref[...] * jnp.concatenate(u_f32, axis=-1)
    y = _gelu(y)
    y = y * jax.nn.sigmoid(jnp.dot(y.astype(BF16), wglu_ref[...], preferred_element_type=F32))
    for kt in range(SSM_LANE_TILES):
        tb_ref[kt] = y[:, kt * LANES:(kt + 1) * LANES]
    for b in range(batch):
        for kt in range(SSM_LANE_TILES):
            c0 = b * SSM_WIDTH + kt * LANES
            y_ref[:, c0:c0 + LANES] = tb_ref[kt, pl.ds(b, SSM_T_BLOCK, stride=batch), :].astype(BF16)


def _s5(u_sb, mats, d_skip, w_glu, batch, seq):
    b_big, c_big, lam_r, lam_i = mats
    rows = SSM_T_BLOCK * batch
    assert seq % SSM_T_BLOCK == 0 and batch % 8 == 0
    n_state = 2 * SSM_TILE_STATE * SSM_LANE_TILES
    blk = pl.BlockSpec((SSM_T_BLOCK, batch * SSM_WIDTH), lambda j: (j, 0))
    return pl.pallas_call(
        functools.partial(_s5_kernel, batch),
        grid=(seq // SSM_T_BLOCK,),
        in_specs=[blk, _resident(b_big.shape), _resident(c_big.shape), _resident(lam_r.shape),
                  _resident(lam_i.shape), _resident((1, SSM_WIDTH)), _resident((SSM_WIDTH, SSM_WIDTH))],
        out_specs=blk,
        out_shape=jax.ShapeDtypeStruct((seq, batch * SSM_WIDTH), BF16),
        scratch_shapes=[pltpu.VMEM((rows, n_state), F32), pltpu.VMEM((batch, n_state), F32),
                        pltpu.VMEM((SSM_LANE_TILES, rows, LANES), F32)],
        compiler_params=pltpu.CompilerParams(
            dimension_semantics=("arbitrary",), vmem_limit_bytes=_vmem_limit(40 * 1024 * 1024)),
        name="s5_scan",
    )(u_sb, b_big, c_big, lam_r, lam_i, d_skip, w_glu)


def _merge_kernel(att_ref, ret_ref, yc_ref, gate_ref, h_ref, wa_ref, wb_ref, wc_ref, wo_ref,
                  g_ref, b_ref, out_ref):
    d = D_MODEL
    for r0 in range(0, MERGE_ROW_TILE, ROW_TILE):
        rows = slice(r0, r0 + ROW_TILE)
        pa = jnp.dot(att_ref[rows, :], wa_ref[...], preferred_element_type=F32)
        pb = jnp.dot(ret_ref[rows, :], wb_ref[...], preferred_element_type=F32)
        pc = jnp.dot(yc_ref[rows, :], wc_ref[...], preferred_element_type=F32)
        merged = (gate_ref[rows, 0:d].astype(F32) * pa + gate_ref[rows, d:2 * d].astype(F32) * pb
                  + gate_ref[rows, 2 * d:3 * d].astype(F32) * pc)
        y = DEEPNORM_ALPHA * h_ref[rows, :] + jnp.dot(merged.astype(BF16), wo_ref[...],
                                                       preferred_element_type=F32)
        out_ref[rows, :] = _layer_norm(y, g_ref[...], b_ref[...])


MERGE_ROW_TILE = 2 * ROW_TILE


def _merge(att_o, ret_o, yc_sb, gates, h, wa, wb, wc, wo, ln_g, ln_b, batch, seq):
    m = h.shape[0]
    tm = MERGE_ROW_TILE
    assert seq % tm == 0
    n_s = seq // tm
    row = lambda w: pl.BlockSpec((tm, w), lambda i: (i, 0))
    return pl.pallas_call(
        _merge_kernel,
        grid=(m // tm,),
        in_specs=[row(ATT_WIDTH), row(RET_V_WIDTH),
                  pl.BlockSpec((tm, SSM_WIDTH), lambda i: (i % n_s, i // n_s)),
                  row(N_BRANCH * D_MODEL), row(D_MODEL),
                  _resident(wa.shape), _resident(wb.shape), _resident(wc.shape), _resident(wo.shape),
                  _resident((1, D_MODEL)), _resident((1, D_MODEL))],
        out_specs=row(D_MODEL),
        out_shape=jax.ShapeDtypeStruct((m, D_MODEL), F32),
        compiler_params=pltpu.CompilerParams(
            dimension_semantics=("arbitrary",), vmem_limit_bytes=_vmem_limit(54 * 1024 * 1024)),
        name="gated_merge",
    )(att_o, ret_o, yc_sb, gates, h, wa, wb, wc, wo, ln_g, ln_b)


FF_T_BLOCK = 64
GELU_C1 = math.sqrt(2.0 / math.pi)
GELU_C2 = 0.044715 * GELU_C1


def _ffn_kernel(batch, h_ref, wup_ref, cw_ref, cb_ref, wdn_ref, g_ref, b_ref, out_ref,
                tb_ref, ext_ref, act_ref):
    tt = FF_T_BLOCK
    rows = tt * batch
    halo = 2 * batch
    n_lane_tiles = D_MODEL // LANES

    @pl.when(pl.program_id(0) == 0)
    def _():
        ext_ref[0:halo, :] = jnp.zeros((halo, 2 * D_FF), F32)

    @pl.when(pl.program_id(0) != 0)
    def _():
        ext_ref[0:halo, :] = ext_ref[rows:rows + halo, :]

    for b in range(batch):
        for c in range(n_lane_tiles):
            tb_ref[c, pl.ds(b, tt, stride=batch), :] = h_ref[b, :, c * LANES:(c + 1) * LANES]
    hb = jnp.concatenate([tb_ref[c] for c in range(n_lane_tiles)], axis=-1).astype(BF16)
    n_chunks = D_FF // FF_CHUNK
    for j in range(n_chunks):
        for col in (j * FF_CHUNK, D_FF + j * FF_CHUNK):
            cols = slice(col, col + FF_CHUNK)
            ext_ref[halo:halo + rows, cols] = jnp.dot(hb, wup_ref[:, cols], preferred_element_type=F32)

    def conv(r0, nr, col):
        cols = slice(col, col + FF_CHUNK)
        w = cw_ref[:, cols]
        return (ext_ref[r0:r0 + nr, cols] * w[0:1] + ext_ref[r0 + batch:r0 + batch + nr, cols] * w[1:2]
                + ext_ref[r0 + halo:r0 + halo + nr, cols] * w[2:3] + cb_ref[:, cols])

    nr = rows // 2
    for r0 in (0, nr):
        for j in range(n_chunks):
            a = conv(r0, nr, j * FF_CHUNK)
            g_half = conv(r0, nr, D_FF + j * FF_CHUNK)
            t = jnp.tanh(a * (GELU_C1 + GELU_C2 * (a * a)))
            act_ref[r0:r0 + nr, j * FF_CHUNK:(j + 1) * FF_CHUNK] = ((a + a * t) * g_half).astype(BF16)
        h = jnp.concatenate([tb_ref[c, r0:r0 + nr, :] for c in range(n_lane_tiles)], axis=-1)
        y = DEEPNORM_ALPHA * h + jnp.dot(act_ref[r0:r0 + nr, :], wdn_ref[...], preferred_element_type=F32)
        y = _layer_norm(y, g_ref[...], b_ref[...])
        for c in range(n_lane_tiles):
            tb_ref[c, r0:r0 + nr, :] = y[:, c * LANES:(c + 1) * LANES]
    for b in range(batch):
        for c in range(n_lane_tiles):
            out_ref[b, :, c * LANES:(c + 1) * LANES] = tb_ref[c, pl.ds(b, tt, stride=batch), :]


def _ffn(h, w_up, conv_w, conv_b, w_down, ln_g, ln_b, batch, seq):
    tt = FF_T_BLOCK
    rows = tt * batch
    assert (2 * D_FF) % COL_CHUNK == 0 and D_FF % FF_CHUNK == 0 and seq % tt == 0 and batch % 8 == 0
    half_gate = jnp.concatenate([jnp.ones((D_FF,), F32), jnp.full((D_FF,), 0.5, F32)])
    blk = pl.BlockSpec((batch, tt, D_MODEL), lambda j: (0, j, 0))
    out = pl.pallas_call(
        functools.partial(_ffn_kernel, batch),
        grid=(seq // tt,),
        in_specs=[blk, _resident(w_up.shape), _resident(conv_w.shape), _resident((1, 2 * D_FF)),
                  _resident(w_down.shape), _resident((1, D_MODEL)), _resident((1, D_MODEL))],
        out_specs=blk,
        out_shape=jax.ShapeDtypeStruct((batch, seq, D_MODEL), F32),
        scratch_shapes=[pltpu.VMEM((D_MODEL // LANES, rows, LANES), F32),
                        pltpu.VMEM((rows + 2 * batch, 2 * D_FF), F32),
                        pltpu.VMEM((rows, D_FF), BF16)],
        compiler_params=pltpu.CompilerParams(
            dimension_semantics=("arbitrary",), vmem_limit_bytes=_vmem_limit(58 * 1024 * 1024)),
        name="conv_ffn",
    )(h.reshape(batch, seq, D_MODEL), w_up, conv_w * half_gate, conv_b * half_gate, w_down, ln_g, ln_b)
    return out.reshape(batch * seq, D_MODEL)


def _rotary_tables(seq):
    half = RET_QK_DIM // 2
    inv = ROPE_BASE ** (-jnp.arange(0, RET_QK_DIM, 2, dtype=F32) / RET_QK_DIM)
    ang = jnp.arange(seq, dtype=F32)[:, None] * inv[None, :]
    cos, sin = jnp.cos(ang), jnp.sin(ang)
    assert cos.shape[1] == half
    cos_t = jnp.tile(jnp.concatenate([cos, cos], -1), (1, RET_HEADS))
    sin_t = jnp.tile(jnp.concatenate([-sin, sin], -1), (1, RET_HEADS))
    return cos_t, sin_t


def kernel(x, ln_in_g, ln_in_b, w_in, rel_bias, w_proj_a, w_proj_b, w_proj_c, lam_re, lam_im, log_step,
           b_re, b_im, c_re, c_im, d_skip, w_glu, w_o, ln1_g, ln1_b, w_up, conv_w, conv_b, w_down,
           ln2_g, ln2_b):
    batch, seq, d = x.shape
    assert d == D_MODEL and seq % ROW_TILE == 0 and w_in.shape[0] == DEPTH
    m = batch * seq
    vec = lambda a: a.reshape(1, -1).astype(F32)
    cos_t, sin_t = _rotary_tables(seq)
    ret_tabs = _ret_tables()
    h = x.reshape(m, d)
    for l in range(DEPTH):
        w_l = w_in[l].astype(BF16)
        if l == 0:
            h, att_qkv, ret_qk, ret_v, ret_g, u_sb, gates = _in_proj(
                h, vec(ln_in_g), vec(ln_in_b), w_l, cos_t, sin_t, batch, seq)
        else:
            att_qkv, ret_qk, ret_v, ret_g, u_sb, gates = _in_proj(
                h, None, None, w_l, cos_t, sin_t, batch, seq)
        att_o = _attention(att_qkv, _att_bias_table(rel_bias[l]), batch, seq)
        ret_o = _retention(ret_qk, ret_v, ret_g, ret_tabs, batch, seq)
        lbr, lbi, bbr, bbi = _s5_discretise(lam_re[l], lam_im[l], log_step[l], b_re[l], b_im[l])
        mats = _s5_matrices(lbr, lbi, bbr, bbi, c_re[l], c_im[l], batch)
        yc_sb = _s5(u_sb, mats, vec(d_skip[l]), w_glu[l].astype(BF16), batch, seq)
        h = _merge(att_o, ret_o, yc_sb, gates, h, w_proj_a[l].astype(BF16), w_proj_b[l].astype(BF16),
                   w_proj_c[l].astype(BF16), w_o[l].astype(BF16), vec(ln1_g[l]), vec(ln1_b[l]), batch, seq)
        h = _ffn(h, w_up[l].astype(BF16), conv_w[l].astype(F32), vec(conv_b[l]), w_down[l].astype(BF16),
                 vec(ln2_g[l]), vec(ln2_b[l]), batch, seq)
    return h.reshape(batch, seq, d)
```

```python
import functools
import math

import jax
import jax.numpy as jnp
from jax import lax
from jax.experimental import pallas as pl
from jax.experimental.pallas import tpu as pltpu

F32 = jnp.float32
BF16 = jnp.bfloat16

D_MODEL = 1024
DEPTH = 2
CHUNK = 64
ATT_HEADS = 8
ATT_HEAD_DIM = 64
ATT_WIDTH = ATT_HEADS * ATT_HEAD_DIM
LEFT_CHUNKS = 8
MAX_REL = 128
RET_HEADS = 8
RET_QK_DIM = 64
RET_V_DIM = 128
RET_QK_WIDTH = RET_HEADS * RET_QK_DIM
RET_V_WIDTH = RET_HEADS * RET_V_DIM
ROPE_BASE = 10000.0
SSM_WIDTH = 512
SSM_GROUP = 16
SSM_GROUPS = SSM_WIDTH // SSM_GROUP
SSM_STATE = 64
D_FF = 2816
N_BRANCH = 3
IN_WIDTH = 8192
DEEPNORM_ALPHA = (2.0 * DEPTH) ** 0.25
LN_EPS = 1e-5
MASK_VALUE = -1e30
LOG2E = math.log2(math.e)

LANES = 128
V7X_VMEM_BYTES = 64 * 1024 * 1024

ROW_TILE = 512
COL_CHUNK = 512
ATT_Q_BLOCK = 256
RET_BLOCK = 256
RET_BATCH_PER_STEP = 4
SSM_T_BLOCK = 64
SSM_BLOCKS_PER_STEP = 2
SSM_LANE_TILES = SSM_WIDTH // LANES
SSM_TILE_STATE = (LANES // SSM_GROUP) * SSM_STATE
FF_CHUNK = 256


def _vmem_limit(nbytes):
    return int(min(nbytes, V7X_VMEM_BYTES - 6 * 1024 * 1024))


def _resident(shape):
    nd = len(shape)
    return pl.BlockSpec(shape, lambda *_: (0,) * nd, pipeline_mode=pl.Buffered(1))


def _layer_norm(x, g, b):
    mu = jnp.mean(x, -1, keepdims=True)
    xc = x - mu
    var = jnp.mean(xc * xc, -1, keepdims=True)
    return xc * lax.rsqrt(var + LN_EPS) * g + b


def _gelu(x):
    return 0.5 * x * (1.0 + jnp.tanh(math.sqrt(2.0 / math.pi) * (x + 0.044715 * (x * x * x))))


def _in_proj_kernel(apply_ln, *refs):
    if apply_ln:
        (x_ref, g_ref, b_ref, w_ref, cos_ref, sin_ref,
         hf_ref, att_ref, rqk_ref, rv_ref, rg_ref, u_ref, gate_ref) = refs
    else:
        (x_ref, w_ref, cos_ref, sin_ref,
         att_ref, rqk_ref, rv_ref, rg_ref, u_ref, gate_ref) = refs
    x = x_ref[...]
    if apply_ln:
        x = _layer_norm(x, g_ref[...], b_ref[...])
        hf_ref[...] = x
    xb = x.astype(BF16)

    def proj(c):
        return jnp.dot(xb, w_ref[:, c * COL_CHUNK:(c + 1) * COL_CHUNK], preferred_element_type=F32)

    lane = lax.broadcasted_iota(jnp.int32, (1, COL_CHUNK), 1)
    first_half = (lane % RET_QK_DIM) < (RET_QK_DIM // 2)
    half = RET_QK_DIM // 2

    cos_t = jnp.concatenate([cos_ref[...]] * (COL_CHUNK // LANES), axis=1)
    sin_t = jnp.concatenate([sin_ref[...]] * (COL_CHUNK // LANES), axis=1)

    def rotary(a):
        swapped = jnp.where(first_half, pltpu.roll(a, COL_CHUNK - half, 1), pltpu.roll(a, half, 1))
        return a * cos_t + swapped * sin_t

    att_ref[:, 0:512] = (proj(0) * (ATT_HEAD_DIM ** -0.5 * LOG2E)).astype(BF16)
    att_ref[:, 512:1024] = proj(1).astype(BF16)
    att_ref[:, 1024:1536] = proj(2).astype(BF16)
    rqk_ref[:, 0:512] = (rotary(proj(3)) * (RET_QK_DIM ** -0.5)).astype(BF16)
    rqk_ref[:, 512:1024] = rotary(proj(4)).astype(BF16)
    rv_ref[:, 0:512] = proj(5).astype(BF16)
    rv_ref[:, 512:1024] = proj(6).astype(BF16)
    for c in range(2):
        a = proj(7 + c)
        rg_ref[:, c * 512:(c + 1) * 512] = (a * jax.nn.sigmoid(a)).astype(BF16)
    u_ref[...] = proj(9).astype(BF16)
    for c in range(6):
        gate_ref[:, c * 512:(c + 1) * 512] = jax.nn.sigmoid(proj(10 + c)).astype(BF16)


def _in_proj(x2d, ln_g, ln_b, w_bf16, cos_t, sin_t, batch, seq):
    m = x2d.shape[0]
    tm = ROW_TILE
    n_s = seq // tm
    apply_ln = ln_g is not None
    row = lambda w: pl.BlockSpec((tm, w), lambda i: (i, 0))
    in_specs = [row(D_MODEL)]
    args = [x2d]
    if apply_ln:
        in_specs += [_resident((1, D_MODEL)), _resident((1, D_MODEL))]
        args += [ln_g, ln_b]
    in_specs += [_resident((D_MODEL, IN_WIDTH)),
                 pl.BlockSpec((tm, LANES), lambda i: (i % n_s, 0)),
                 pl.BlockSpec((tm, LANES), lambda i: (i % n_s, 0))]
    args += [w_bf16, cos_t, sin_t]
    out_shape, out_specs = [], []
    if apply_ln:
        out_shape.append(jax.ShapeDtypeStruct((m, D_MODEL), F32))
        out_specs.append(row(D_MODEL))
    for w in (3 * ATT_WIDTH, 2 * RET_QK_WIDTH, RET_V_WIDTH, RET_V_WIDTH):
        out_shape.append(jax.ShapeDtypeStruct((m, w), BF16))
        out_specs.append(row(w))
    out_shape.append(jax.ShapeDtypeStruct((seq, batch * SSM_WIDTH), BF16))
    out_specs.append(pl.BlockSpec((tm, SSM_WIDTH), lambda i: (i % n_s, i // n_s)))
    out_shape.append(jax.ShapeDtypeStruct((m, N_BRANCH * D_MODEL), BF16))
    out_specs.append(row(N_BRANCH * D_MODEL))
    return pl.pallas_call(
        functools.partial(_in_proj_kernel, apply_ln),
        grid=(m // tm,),
        in_specs=in_specs,
        out_specs=out_specs,
        out_shape=out_shape,
        compiler_params=pltpu.CompilerParams(
            dimension_semantics=("arbitrary",), vmem_limit_bytes=_vmem_limit(56 * 1024 * 1024)),
        name="in_proj_ln" if apply_ln else "in_proj",
    )(*args)


ATT_Q_PER_STEP = 4


def _att_kernel(q_ref, *refs):
    n_kv = ATT_Q_PER_STEP + 2
    k_refs, v_refs, (gen_ref, o_ref, bias_ref) = refs[:n_kv], refs[n_kv:2 * n_kv], refs[2 * n_kv:]
    tq = ATT_Q_BLOCK

    @pl.when((pl.program_id(0) == 0) & (pl.program_id(1) == 0))
    def _():
        qc = lax.broadcasted_iota(jnp.int32, (tq, 3 * tq), 0) // CHUNK
        kc = lax.broadcasted_iota(jnp.int32, (tq, 3 * tq), 1) // CHUNK
        in_band = (kc >= qc) & (kc <= qc + LEFT_CHUNKS)
        for h in range(ATT_HEADS):
            gen = jnp.broadcast_to(gen_ref[h:h + 1, :], (tq, ATT_BIAS_PERIOD))
            toep = pltpu.roll(gen, 0, 1, stride=1, stride_axis=0)
            bias_ref[h] = jnp.where(in_band, toep[:, :3 * tq], MASK_VALUE)

    k_all = jnp.concatenate([r[...] for r in k_refs], axis=0)
    v_all = jnp.concatenate([r[...] for r in v_refs], axis=0)
    lane = lax.broadcasted_iota(jnp.int32, (1, LANES), 1)
    first = (lane == 0).astype(BF16)
    row = lax.broadcasted_iota(jnp.int32, (3 * tq, 1), 0)
    q_one = jnp.broadcast_to(first, (tq, LANES))
    v_ones = jnp.ones((3 * tq, LANES), BF16)
    for sub in range(ATT_Q_PER_STEP):
        qb = pl.program_id(1) * ATT_Q_PER_STEP + sub
        q = q_ref[sub * tq:(sub + 1) * tq, :]
        k = k_all[sub * tq:(sub + 3) * tq, :]
        v = v_all[sub * tq:(sub + 3) * tq, :]
        pen = jnp.where(row < tq, jnp.where(qb >= 2, 0.0, MASK_VALUE),
                        jnp.where(row < 2 * tq, jnp.where(qb >= 1, 0.0, MASK_VALUE), 0.0))
        k_pen = pen.astype(BF16) * first
        for p in range(ATT_WIDTH // LANES):
            qp = q[:, p * LANES:(p + 1) * LANES]
            k_ext = jnp.concatenate([k[:, p * LANES:(p + 1) * LANES], k_pen], axis=1)
            v_ext = jnp.concatenate([v[:, p * LANES:(p + 1) * LANES], v_ones], axis=1)
            outs = []
            for hh in range(2):
                h = 2 * p + hh
                head_lanes = ((lane // ATT_HEAD_DIM) == hh).astype(BF16)
                q_ext = jnp.concatenate([qp * head_lanes, q_one], axis=1)
                s = lax.dot_general(q_ext, k_ext, (((1,), (1,)), ((), ())), preferred_element_type=F32)
                s = s + bias_ref[h]
                e = jnp.exp2(s - jnp.max(s, -1, keepdims=True))
                o = jnp.dot(e.astype(BF16), v_ext, preferred_element_type=F32)
                outs.append(o[:, :LANES] / o[:, LANES:])
            o_ref[sub * tq:(sub + 1) * tq, p * LANES:(p + 1) * LANES] = jnp.where(
                (lane // ATT_HEAD_DIM) == 0, outs[0], outs[1]).astype(BF16)


ATT_BIAS_PERIOD = 4 * ATT_Q_BLOCK


def _att_bias_generator(rel_bias):
    tq = ATT_Q_BLOCK
    rb = rel_bias.astype(F32) * LOG2E
    nh = rb.shape[0]
    fill = lambda col, n: jnp.broadcast_to(rb[:, col:col + 1], (nh, n))
    g = jnp.concatenate([fill(2 * MAX_REL, 2 * tq - MAX_REL), rb[:, ::-1],
                         fill(0, tq - MAX_REL - 1), fill(2 * MAX_REL, tq)], axis=1)
    assert g.shape[1] == ATT_BIAS_PERIOD
    return g


def _attention(att_qkv, bias_gen, batch, seq):
    m = att_qkv.shape[0]
    tq = ATT_Q_BLOCK
    n_sub = ATT_Q_PER_STEP
    assert LEFT_CHUNKS * CHUNK == 2 * tq and seq % (n_sub * tq) == 0
    nq = seq // tq
    n_steps = nq // n_sub
    qspec = pl.BlockSpec((n_sub * tq, ATT_WIDTH), lambda b, i: (b * n_steps + i, 0))

    def kv(col, off):
        return pl.BlockSpec((tq, ATT_WIDTH),
                            lambda b, i: (b * nq + jnp.maximum(i * n_sub - 2 + off, 0), col))

    n_kv = n_sub + 2
    return pl.pallas_call(
        _att_kernel,
        grid=(batch, n_steps),
        in_specs=([qspec] + [kv(1, j) for j in range(n_kv)] + [kv(2, j) for j in range(n_kv)]
                  + [_resident((ATT_HEADS, ATT_BIAS_PERIOD))]),
        out_specs=pl.BlockSpec((n_sub * tq, ATT_WIDTH), lambda b, i: (b * n_steps + i, 0)),
        out_shape=jax.ShapeDtypeStruct((m, ATT_WIDTH), BF16),
        scratch_shapes=[pltpu.VMEM((ATT_HEADS, tq, 3 * tq), F32)],
        compiler_params=pltpu.CompilerParams(
            dimension_semantics=("arbitrary", "arbitrary"),
            vmem_limit_bytes=_vmem_limit(40 * 1024 * 1024)),
        name="band_attention",
    )(*([att_qkv] * (1 + 2 * n_kv)), bias_gen)


def _ret_kernel(q_ref, k_ref, v_ref, g_ref, dm_ref, qdec_ref, kdec_ref, gblk_ref, o_ref, state_ref):
    @pl.when(pl.program_id(1) == 0)
    def _():
        state_ref[...] = jnp.zeros_like(state_ref)

    lane = lax.broadcasted_iota(jnp.int32, (1, LANES), 1)
    sub = lax.broadcasted_iota(jnp.int32, (LANES, 1), 0)
    for sb in range(RET_BATCH_PER_STEP):
        q = q_ref[sb]
        k = k_ref[sb]
        kd = k.astype(F32) * kdec_ref[...]
        for p in range(RET_QK_WIDTH // LANES):
            qp = q[:, p * LANES:(p + 1) * LANES]
            kp = k[:, p * LANES:(p + 1) * LANES]
            kd_t = kd[:, p * LANES:(p + 1) * LANES].T
            st = state_ref[sb, p]
            st_b = st.astype(BF16)
            new_st = st * gblk_ref[p]
            for hh in range(2):
                h = 2 * p + hh
                qm = qp * ((lane // RET_QK_DIM) == hh).astype(BF16)
                s = lax.dot_general(qm, kp, (((1,), (1,)), ((), ())), preferred_element_type=F32)
                sd = (s * dm_ref[h]).astype(BF16)
                vh = v_ref[sb, :, h * RET_V_DIM:(h + 1) * RET_V_DIM]
                inner = jnp.dot(sd, vh, preferred_element_type=F32)
                cross = jnp.dot(qm, st_b, preferred_element_type=F32) * qdec_ref[h]
                out = inner + cross
                mu = jnp.mean(out, -1, keepdims=True)
                oc = out - mu
                var = jnp.mean(oc * oc, -1, keepdims=True)
                nrm = oc * lax.rsqrt(var + LN_EPS)
                gate = g_ref[sb, :, h * RET_V_DIM:(h + 1) * RET_V_DIM].astype(F32)
                o_ref[sb, :, h * RET_V_DIM:(h + 1) * RET_V_DIM] = (nrm * gate).astype(BF16)
                kd_h = jnp.where((sub // RET_QK_DIM) == hh, kd_t, 0.0).astype(BF16)
                new_st = new_st + jnp.dot(kd_h, vh, preferred_element_type=F32)
            state_ref[sb, p] = new_st


def _ret_tables():
    t = RET_BLOCK
    log_g = jnp.log(1.0 - jnp.power(2.0, -5.0 - jnp.arange(RET_HEADS, dtype=F32)))
    pos = jnp.arange(t, dtype=F32)
    n, mm = pos[:, None], pos[None, :]
    cn, cm = jnp.floor(n / CHUNK), jnp.floor(mm / CHUNK)
    expo = jnp.where(cn == cm, jnp.abs(n - mm), n - mm)
    dm = jnp.where((cm <= cn)[None], jnp.exp(log_g[:, None, None] * jnp.where(cm <= cn, expo, 0.0)[None]), 0.0)
    qdec = jnp.exp(log_g[:, None] * (pos + 1.0)[None, :])
    qdec = jnp.broadcast_to(qdec[:, :, None], (RET_HEADS, t, RET_V_DIM))
    kdec = jnp.exp(log_g[:, None] * (t - 1.0 - pos)[None, :])
    kdec = jnp.repeat(kdec.T, RET_QK_DIM, axis=1)
    gblk = jnp.repeat(jnp.exp(log_g * t), RET_QK_DIM).reshape(RET_QK_WIDTH // LANES, LANES, 1)
    gblk = jnp.broadcast_to(gblk, (RET_QK_WIDTH // LANES, LANES, RET_V_DIM))
    return dm, qdec, kdec, gblk


def _retention(ret_qk, ret_v, ret_g, tables, batch, seq):
    m = ret_qk.shape[0]
    t = RET_BLOCK
    nsb = RET_BATCH_PER_STEP
    assert seq % t == 0 and t % CHUNK == 0 and batch % nsb == 0
    dm, qdec, kdec, gblk = tables
    n_pairs = RET_QK_WIDTH // LANES
    rows = lambda w, col: pl.BlockSpec((nsb, t, w), lambda b, i: (b, i, col))
    as3d = lambda a: a.reshape(batch, seq, a.shape[-1])
    out = pl.pallas_call(
        _ret_kernel,
        grid=(batch // nsb, seq // t),
        in_specs=[rows(RET_QK_WIDTH, 0), rows(RET_QK_WIDTH, 1), rows(RET_V_WIDTH, 0), rows(RET_V_WIDTH, 0),
                  _resident((RET_HEADS, t, t)), _resident((RET_HEADS, t, RET_V_DIM)),
                  _resident((t, RET_QK_WIDTH)), _resident((n_pairs, LANES, RET_V_DIM))],
        out_specs=rows(RET_V_WIDTH, 0),
        out_shape=jax.ShapeDtypeStruct((batch, seq, RET_V_WIDTH), BF16),
        scratch_shapes=[pltpu.VMEM((nsb, n_pairs, LANES, RET_V_DIM), F32)],
        compiler_params=pltpu.CompilerParams(
            dimension_semantics=("arbitrary", "arbitrary"),
            vmem_limit_bytes=_vmem_limit(40 * 1024 * 1024)),
        name="chunk_retention",
    )(as3d(ret_qk), as3d(ret_qk), as3d(ret_v), as3d(ret_g), dm, qdec, kdec, gblk)
    return out.reshape(m, RET_V_WIDTH)


def _s5_disc_kernel(lre_ref, lim_ref, lstep_ref, bre_ref, bim_ref, lbr_ref, lbi_ref, bbr_ref, bbi_ref):
    lre, lim = lre_ref[...], lim_ref[...]
    step = jnp.exp(lstep_ref[...])
    mag = jnp.exp(lre * step)
    ang = lim * step
    lbr = mag * jnp.cos(ang)
    lbi = mag * jnp.sin(ang)
    den = lre * lre + lim * lim
    nr = lbr - 1.0
    cr = (nr * lre + lbi * lim) / den
    ci = (lbi * lre - nr * lim) / den
    lbr_ref[...] = lbr
    lbi_ref[...] = lbi
    bre, bim = bre_ref[...], bim_ref[...]
    bbr_ref[...] = cr * bre - ci * bim
    bbi_ref[...] = cr * bim + ci * bre


def _s5_discretise(lam_re, lam_im, log_step, b_re, b_im):
    gp = SSM_GROUPS * SSM_STATE
    row = lambda a: a.reshape(1, gp).astype(F32)
    lstep = jnp.repeat(log_step.astype(F32), SSM_STATE).reshape(1, gp)
    b_t = lambda b: b.astype(F32).reshape(gp, SSM_GROUP).T
    full = lambda s: pl.BlockSpec(s, lambda: (0,) * len(s))
    return pl.pallas_call(
        _s5_disc_kernel,
        in_specs=[full((1, gp))] * 3 + [full((SSM_GROUP, gp))] * 2,
        out_specs=[full((1, gp))] * 2 + [full((SSM_GROUP, gp))] * 2,
        out_shape=[jax.ShapeDtypeStruct((1, gp), F32)] * 2 + [jax.ShapeDtypeStruct((SSM_GROUP, gp), F32)] * 2,
        name="s5_discretise",
    )(row(lam_re), row(lam_im), lstep, b_t(b_re), b_t(b_im))


def _s5_matrices(lbr, lbi, bbr, bbi, c_re, c_im, batch):
    nt, gl = SSM_LANE_TILES, LANES // SSM_GROUP
    eye = jnp.eye(gl, dtype=F32)

    def expand(bb):
        bb = bb.reshape(SSM_GROUP, nt, gl, SSM_STATE).transpose(1, 2, 0, 3)
        return jnp.einsum('tgip,gh->tgihp', bb, eye).reshape(nt, LANES, SSM_TILE_STATE)

    def contract(c):
        c = c.astype(F32).reshape(nt, gl, SSM_GROUP, SSM_STATE)
        return jnp.einsum('tgop,gh->tgpho', c, eye).reshape(nt, SSM_TILE_STATE, LANES)

    b_big = jnp.concatenate([expand(bbr), expand(bbi)], axis=-1).astype(BF16)
    c_big = jnp.concatenate([contract(c_re), -contract(c_im)], axis=1).astype(BF16)
    lam_r = jnp.broadcast_to(lbr, (batch, SSM_GROUPS * SSM_STATE))
    lam_i = jnp.broadcast_to(lbi, (batch, SSM_GROUPS * SSM_STATE))
    return b_big, c_big, lam_r, lam_i


def _s5_kernel(batch, u_ref, bbig_ref, cbig_ref, lr_ref, li_ref, dskip_ref, wglu_ref, y_ref,
               x_ref, st_ref, tb_ref):
    @pl.when(pl.program_id(0) == 0)
    def _():
        st_ref[...] = jnp.zeros_like(st_ref)

    ts = SSM_TILE_STATE
    tt = SSM_T_BLOCK
    blocks = range(SSM_BLOCKS_PER_STEP)
    tiles = range(SSM_LANE_TILES)
    for blk in blocks:
        for b in range(batch):
            for kt in tiles:
                c0 = b * SSM_WIDTH + kt * LANES
                tb_ref[blk, kt, pl.ds(b, tt, stride=batch), :] = (
                    u_ref[blk * tt:(blk + 1) * tt, c0:c0 + LANES].astype(F32))
    u_f32 = [[tb_ref[blk, kt] for kt in tiles] for blk in blocks]
    for blk in blocks:
        for kt in tiles:
            x_ref[blk, :, 2 * ts * kt:2 * ts * (kt + 1)] = jnp.dot(
                u_f32[blk][kt].astype(BF16), bbig_ref[kt], preferred_element_type=F32)

    for blk in blocks:
        for kt in tiles:
            re0, im0 = 2 * ts * kt, 2 * ts * kt + ts
            lr = lr_ref[:, kt * ts:(kt + 1) * ts]
            li = li_ref[:, kt * ts:(kt + 1) * ts]
            xr, xi = st_ref[:, re0:re0 + ts], st_ref[:, im0:im0 + ts]
            for t in range(tt):
                rows = slice(t * batch, (t + 1) * batch)
                xr, xi = (lr * xr - li * xi + x_ref[blk, rows, re0:re0 + ts],
                          lr * xi + li * xr + x_ref[blk, rows, im0:im0 + ts])
                x_ref[blk, rows, re0:re0 + ts] = xr
                x_ref[blk, rows, im0:im0 + ts] = xi
            st_ref[:, re0:re0 + ts] = xr
            st_ref[:, im0:im0 + ts] = xi

    for blk in blocks:
        ys = []
        for kt in tiles:
            xs = x_ref[blk, :, 2 * ts * kt:2 * ts * (kt + 1)].astype(BF16)
            ys.append(jnp.dot(xs, cbig_ref[kt], preferred_element_type=F32))
        y = jnp.concatenate(ys, axis=-1) + dskip_ref[...] * jnp.concatenate(u_f32[blk], axis=-1)
        y = _gelu(y)
        y = y * jax.nn.sigmoid(jnp.dot(y.astype(BF16), wglu_ref[...], preferred_element_type=F32))
        for kt in tiles:
            tb_ref[blk, kt] = y[:, kt * LANES:(kt + 1) * LANES]
        for b in range(batch):
            for kt in tiles:
                c0 = b * SSM_WIDTH + kt * LANES
                y_ref[blk * tt:(blk + 1) * tt, c0:c0 + LANES] = (
                    tb_ref[blk, kt, pl.ds(b, tt, stride=batch), :].astype(BF16))


def _s5(u_sb, mats, d_skip, w_glu, batch, seq):
    b_big, c_big, lam_r, lam_i = mats
    nblk = SSM_BLOCKS_PER_STEP
    rows = SSM_T_BLOCK * batch
    assert seq % (nblk * SSM_T_BLOCK) == 0 and batch % 8 == 0
    n_state = 2 * SSM_TILE_STATE * SSM_LANE_TILES
    blk = pl.BlockSpec((nblk * SSM_T_BLOCK, batch * SSM_WIDTH), lambda j: (j, 0))
    return pl.pallas_call(
        functools.partial(_s5_kernel, batch),
        grid=(seq // (nblk * SSM_T_BLOCK),),
        in_specs=[blk, _resident(b_big.shape), _resident(c_big.shape), _resident(lam_r.shape),
                  _resident(lam_i.shape), _resident((1, SSM_WIDTH)), _resident((SSM_WIDTH, SSM_WIDTH))],
        out_specs=blk,
        out_shape=jax.ShapeDtypeStruct((seq, batch * SSM_WIDTH), BF16),
        scratch_shapes=[pltpu.VMEM((nblk, rows, n_state), F32), pltpu.VMEM((batch, n_state), F32),
                        pltpu.VMEM((nblk, SSM_LANE_TILES, rows, LANES), F32)],
        compiler_params=pltpu.CompilerParams(
            dimension_semantics=("arbitrary",), vmem_limit_bytes=_vmem_limit(40 * 1024 * 1024)),
        name="s5_scan",
    )(u_sb, b_big, c_big, lam_r, lam_i, d_skip, w_glu)


def _merge_kernel(att_ref, ret_ref, yc_ref, gate_ref, h_ref, wa_ref, wb_ref, wc_ref, wo_ref,
                  g_ref, b_ref, out_ref):
    d = D_MODEL
    for r0 in range(0, MERGE_ROW_TILE, MERGE_ROW_PART):
        rows = slice(r0, r0 + MERGE_ROW_PART)
        pa = jnp.dot(att_ref[rows, :], wa_ref[...], preferred_element_type=F32)
        pb = jnp.dot(ret_ref[rows, :], wb_ref[...], preferred_element_type=F32)
        pc = jnp.dot(yc_ref[rows, :], wc_ref[...], preferred_element_type=F32)
        merged = (gate_ref[rows, 0:d].astype(F32) * pa + gate_ref[rows, d:2 * d].astype(F32) * pb
                  + gate_ref[rows, 2 * d:3 * d].astype(F32) * pc)
        y = DEEPNORM_ALPHA * h_ref[rows, :] + jnp.dot(merged.astype(BF16), wo_ref[...],
                                                       preferred_element_type=F32)
        out_ref[rows, :] = _layer_norm(y, g_ref[...], b_ref[...])


MERGE_ROW_TILE = 2 * ROW_TILE
MERGE_ROW_PART = ROW_TILE // 2


def _merge(att_o, ret_o, yc_sb, gates, h, wa, wb, wc, wo, ln_g, ln_b, batch, seq):
    m = h.shape[0]
    tm = MERGE_ROW_TILE
    assert seq % tm == 0
    n_s = seq // tm
    row = lambda w: pl.BlockSpec((tm, w), lambda i: (i, 0))
    return pl.pallas_call(
        _merge_kernel,
        grid=(m // tm,),
        in_specs=[row(ATT_WIDTH), row(RET_V_WIDTH),
                  pl.BlockSpec((tm, SSM_WIDTH), lambda i: (i % n_s, i // n_s)),
                  row(N_BRANCH * D_MODEL), row(D_MODEL),
                  _resident(wa.shape), _resident(wb.shape), _resident(wc.shape), _resident(wo.shape),
                  _resident((1, D_MODEL)), _resident((1, D_MODEL))],
        out_specs=row(D_MODEL),
        out_shape=jax.ShapeDtypeStruct((m, D_MODEL), F32),
        compiler_params=pltpu.CompilerParams(
            dimension_semantics=("arbitrary",), vmem_limit_bytes=_vmem_limit(54 * 1024 * 1024)),
        name="gated_merge",
    )(att_o, ret_o, yc_sb, gates, h, wa, wb, wc, wo, ln_g, ln_b)


FF_T_BLOCK = 64
FF_ROW_PARTS = 4
GELU_C1 = math.sqrt(2.0 / math.pi)
GELU_C2 = 0.044715 * GELU_C1


def _ffn_kernel(batch, h_ref, wup_ref, cw_ref, cb_ref, wdn_ref, g_ref, b_ref, out_ref,
                tb_ref, ext_ref, act_ref):
    tt = FF_T_BLOCK
    rows = tt * batch
    halo = 2 * batch
    n_lane_tiles = D_MODEL // LANES

    @pl.when(pl.program_id(0) == 0)
    def _():
        ext_ref[0:halo, :] = jnp.zeros((halo, 2 * D_FF), F32)

    @pl.when(pl.program_id(0) != 0)
    def _():
        ext_ref[0:halo, :] = ext_ref[rows:rows + halo, :]

    for b in range(batch):
        for c in range(n_lane_tiles):
            tb_ref[c, pl.ds(b, tt, stride=batch), :] = h_ref[b, :, c * LANES:(c + 1) * LANES]
    hb = jnp.concatenate([tb_ref[c] for c in range(n_lane_tiles)], axis=-1).astype(BF16)
    n_chunks = D_FF // FF_CHUNK

    def up_proj(j):
        for col in (j * FF_CHUNK, D_FF + j * FF_CHUNK):
            cols = slice(col, col + FF_CHUNK)
            ext_ref[halo:halo + rows, cols] = jnp.dot(hb, wup_ref[:, cols], preferred_element_type=F32)

    def conv(col):
        cols = slice(col, col + FF_CHUNK)
        w = cw_ref[:, cols]
        return (ext_ref[0:rows, cols] * w[0:1] + ext_ref[batch:batch + rows, cols] * w[1:2]
                + ext_ref[halo:halo + rows, cols] * w[2:3] + cb_ref[:, cols])

    def conv_gate(j):
        a = conv(j * FF_CHUNK)
        g_half = conv(D_FF + j * FF_CHUNK)
        t = jnp.tanh(a * (GELU_C1 + GELU_C2 * (a * a)))
        act_ref[:, j * FF_CHUNK:(j + 1) * FF_CHUNK] = ((a + a * t) * g_half).astype(BF16)

    up_proj(0)
    for j in range(n_chunks):
        if j + 1 < n_chunks:
            up_proj(j + 1)
        conv_gate(j)

    tp = tt // FF_ROW_PARTS
    nr = tp * batch
    for part in range(FF_ROW_PARTS):
        r0 = part * nr
        h = jnp.concatenate([tb_ref[c, r0:r0 + nr, :] for c in range(n_lane_tiles)], axis=-1)
        y = DEEPNORM_ALPHA * h + jnp.dot(act_ref[r0:r0 + nr, :], wdn_ref[...], preferred_element_type=F32)
        y = _layer_norm(y, g_ref[...], b_ref[...])
        for c in range(n_lane_tiles):
            tb_ref[c, r0:r0 + nr, :] = y[:, c * LANES:(c + 1) * LANES]
        for b in range(batch):
            for c in range(n_lane_tiles):
                out_ref[b, part * tp:(part + 1) * tp, c * LANES:(c + 1) * LANES] = (
                    tb_ref[c, pl.ds(r0 + b, tp, stride=batch), :])


def _ffn(h, w_up, conv_w, conv_b, w_down, ln_g, ln_b, batch, seq):
    tt = FF_T_BLOCK
    rows = tt * batch
    assert (2 * D_FF) % COL_CHUNK == 0 and D_FF % FF_CHUNK == 0 and seq % tt == 0 and batch % 8 == 0
    half_gate = jnp.concatenate([jnp.ones((D_FF,), F32), jnp.full((D_FF,), 0.5, F32)])
    blk = pl.BlockSpec((batch, tt, D_MODEL), lambda j: (0, j, 0))
    out = pl.pallas_call(
        functools.partial(_ffn_kernel, batch),
        grid=(seq // tt,),
        in_specs=[blk, _resident(w_up.shape), _resident(conv_w.shape), _resident((1, 2 * D_FF)),
                  _resident(w_down.shape), _resident((1, D_MODEL)), _resident((1, D_MODEL))],
        out_specs=blk,
        out_shape=jax.ShapeDtypeStruct((batch, seq, D_MODEL), F32),
        scratch_shapes=[pltpu.VMEM((D_MODEL // LANES, rows, LANES), F32),
                        pltpu.VMEM((rows + 2 * batch, 2 * D_FF), F32),
                        pltpu.VMEM((rows, D_FF), BF16)],
        compiler_params=pltpu.CompilerParams(
            dimension_semantics=("arbitrary",), vmem_limit_bytes=_vmem_limit(58 * 1024 * 1024)),
        name="conv_ffn",
    )(h.reshape(batch, seq, D_MODEL), w_up, conv_w * half_gate, conv_b * half_gate, w_down, ln_g, ln_b)
    return out.reshape(batch * seq, D_MODEL)


def _rotary_tables(seq):
    half = RET_QK_DIM // 2
    inv = ROPE_BASE ** (-jnp.arange(0, RET_QK_DIM, 2, dtype=F32) / RET_QK_DIM)
    ang = jnp.arange(seq, dtype=F32)[:, None] * inv[None, :]
    cos, sin = jnp.cos(ang), jnp.sin(ang)
    assert cos.shape[1] == half
    heads_per_tile = LANES // RET_QK_DIM
    cos_t = jnp.tile(jnp.concatenate([cos, cos], -1), (1, heads_per_tile))
    sin_t = jnp.tile(jnp.concatenate([-sin, sin], -1), (1, heads_per_tile))
    return cos_t, sin_t


def kernel(x, ln_in_g, ln_in_b, w_in, rel_bias, w_proj_a, w_proj_b, w_proj_c, lam_re, lam_im, log_step,
           b_re, b_im, c_re, c_im, d_skip, w_glu, w_o, ln1_g, ln1_b, w_up, conv_w, conv_b, w_down,
           ln2_g, ln2_b):
    batch, seq, d = x.shape
    assert d == D_MODEL and seq % ROW_TILE == 0 and w_in.shape[0] == DEPTH
    m = batch * seq
    vec = lambda a: a.reshape(1, -1).astype(F32)
    cos_t, sin_t = _rotary_tables(seq)
    ret_tabs = _ret_tables()
    h = x.reshape(m, d)
    for l in range(DEPTH):
        w_l = w_in[l].astype(BF16)
        if l == 0:
            h, att_qkv, ret_qk, ret_v, ret_g, u_sb, gates = _in_proj(
                h, vec(ln_in_g), vec(ln_in_b), w_l, cos_t, sin_t, batch, seq)
        else:
            att_qkv, ret_qk, ret_v, ret_g, u_sb, gates = _in_proj(
                h, None, None, w_l, cos_t, sin_t, batch, seq)
        att_o = _attention(att_qkv, _att_bias_generator(rel_bias[l]), batch, seq)
        ret_o = _retention(ret_qk, ret_v, ret_g, ret_tabs, batch, seq)
        lbr, lbi, bbr, bbi = _s5_discretise(lam_re[l], lam_im[l], log_step[l], b_re[l], b_im[l])
        mats = _s5_matrices(lbr, lbi, bbr, bbi, c_re[l], c_im[l], batch)
        yc_sb = _s5(u_sb, mats, vec(d_skip[l]), w_glu[l].astype(BF16), batch, seq)
        h = _merge(att_o, ret_o, yc_sb, gates, h, w_proj_a[l].astype(BF16), w_proj_b[l].astype(BF16),
                   w_proj_c[l].astype(BF16), w_o[l].astype(BF16), vec(ln1_g[l]), vec(ln1_b[l]), batch, seq)
        h = _ffn(h, w_up[l].astype(BF16), conv_w[l].astype(F32), vec(conv_b[l]), w_down[l].astype(BF16),
                 vec(ln2_g[l]), vec(ln2_b[l]), batch, seq)
    return h.reshape(batch, seq, d)
```

```python
import functools
import math

import jax
import jax.numpy as jnp
from jax import lax
from jax.experimental import pallas as pl
from jax.experimental.pallas import tpu as pltpu

F32 = jnp.float32
BF16 = jnp.bfloat16

D_MODEL = 1024
DEPTH = 2
CHUNK = 64
ATT_HEADS = 8
ATT_HEAD_DIM = 64
ATT_WIDTH = ATT_HEADS * ATT_HEAD_DIM
LEFT_CHUNKS = 8
MAX_REL = 128
RET_HEADS = 8
RET_QK_DIM = 64
RET_V_DIM = 128
RET_QK_WIDTH = RET_HEADS * RET_QK_DIM
RET_V_WIDTH = RET_HEADS * RET_V_DIM
ROPE_BASE = 10000.0
SSM_WIDTH = 512
SSM_GROUP = 16
SSM_GROUPS = SSM_WIDTH // SSM_GROUP
SSM_STATE = 64
D_FF = 2816
N_BRANCH = 3
IN_WIDTH = 8192
MIX_WIDTH = IN_WIDTH - N_BRANCH * D_MODEL
DEEPNORM_ALPHA = (2.0 * DEPTH) ** 0.25
LN_EPS = 1e-5
MASK_VALUE = -1e30
LOG2E = math.log2(math.e)

LANES = 128
V7X_VMEM_BYTES = 64 * 1024 * 1024

ROW_TILE = 512
IN_ROW_TILE = 1024
COL_CHUNK = 512
ATT_Q_BLOCK = 256
RET_BLOCK = 256
RET_BATCH_PER_STEP = 4
SSM_T_BLOCK = 64
SSM_BLOCKS_PER_STEP = 2
SSM_LANE_TILES = SSM_WIDTH // LANES
SSM_TILE_STATE = (LANES // SSM_GROUP) * SSM_STATE
FF_CHUNK = 256


def _vmem_limit(nbytes):
    return int(min(nbytes, V7X_VMEM_BYTES - 6 * 1024 * 1024))


def _resident(shape):
    nd = len(shape)
    return pl.BlockSpec(shape, lambda *_: (0,) * nd, pipeline_mode=pl.Buffered(1))


def _layer_norm(x, g, b):
    mu = jnp.mean(x, -1, keepdims=True)
    xc = x - mu
    var = jnp.mean(xc * xc, -1, keepdims=True)
    return xc * lax.rsqrt(var + LN_EPS) * g + b


def _gelu(x):
    return 0.5 * x * (1.0 + jnp.tanh(math.sqrt(2.0 / math.pi) * (x + 0.044715 * (x * x * x))))


def _in_proj_kernel(apply_ln, *refs):
    if apply_ln:
        (x_ref, g_ref, b_ref, w_ref, cos_ref, sin_ref,
         hf_ref, att_ref, rqk_ref, rv_ref, rg_ref, u_ref) = refs
    else:
        (x_ref, w_ref, cos_ref, sin_ref,
         att_ref, rqk_ref, rv_ref, rg_ref, u_ref) = refs
    x = x_ref[...]
    if apply_ln:
        x = _layer_norm(x, g_ref[...], b_ref[...])
        hf_ref[...] = x
    xb = x.astype(BF16)

    def proj(c):
        return jnp.dot(xb, w_ref[:, c * COL_CHUNK:(c + 1) * COL_CHUNK], preferred_element_type=F32)

    lane = lax.broadcasted_iota(jnp.int32, (1, COL_CHUNK), 1)
    first_half = (lane % RET_QK_DIM) < (RET_QK_DIM // 2)
    half = RET_QK_DIM // 2

    cos_t = jnp.concatenate([cos_ref[...]] * (COL_CHUNK // LANES), axis=1)
    sin_t = jnp.concatenate([sin_ref[...]] * (COL_CHUNK // LANES), axis=1)

    def rotary(a):
        swapped = jnp.where(first_half, pltpu.roll(a, COL_CHUNK - half, 1), pltpu.roll(a, half, 1))
        return a * cos_t + swapped * sin_t

    att_ref[:, 0:512] = (proj(0) * (ATT_HEAD_DIM ** -0.5 * LOG2E)).astype(BF16)
    att_ref[:, 512:1024] = proj(1).astype(BF16)
    att_ref[:, 1024:1536] = proj(2).astype(BF16)
    rqk_ref[:, 0:512] = (rotary(proj(3)) * (RET_QK_DIM ** -0.5)).astype(BF16)
    rqk_ref[:, 512:1024] = rotary(proj(4)).astype(BF16)
    rv_ref[:, 0:512] = proj(5).astype(BF16)
    rv_ref[:, 512:1024] = proj(6).astype(BF16)
    for c in range(2):
        a = proj(7 + c)
        rg_ref[:, c * 512:(c + 1) * 512] = (a * jax.nn.sigmoid(a)).astype(BF16)
    u_ref[...] = proj(9).astype(BF16)


def _in_proj(x2d, ln_g, ln_b, w_bf16, cos_t, sin_t, batch, seq):
    m = x2d.shape[0]
    tm = IN_ROW_TILE
    assert seq % tm == 0 and w_bf16.shape == (D_MODEL, MIX_WIDTH)
    n_s = seq // tm
    apply_ln = ln_g is not None
    row = lambda w: pl.BlockSpec((tm, w), lambda i: (i, 0))
    in_specs = [row(D_MODEL)]
    args = [x2d]
    if apply_ln:
        in_specs += [_resident((1, D_MODEL)), _resident((1, D_MODEL))]
        args += [ln_g, ln_b]
    in_specs += [_resident((D_MODEL, MIX_WIDTH)),
                 pl.BlockSpec((tm, LANES), lambda i: (i % n_s, 0)),
                 pl.BlockSpec((tm, LANES), lambda i: (i % n_s, 0))]
    args += [w_bf16, cos_t, sin_t]
    out_shape, out_specs = [], []
    if apply_ln:
        out_shape.append(jax.ShapeDtypeStruct((m, D_MODEL), F32))
        out_specs.append(row(D_MODEL))
    for w in (3 * ATT_WIDTH, 2 * RET_QK_WIDTH, RET_V_WIDTH, RET_V_WIDTH):
        out_shape.append(jax.ShapeDtypeStruct((m, w), BF16))
        out_specs.append(row(w))
    out_shape.append(jax.ShapeDtypeStruct((seq, batch * SSM_WIDTH), BF16))
    out_specs.append(pl.BlockSpec((tm, SSM_WIDTH), lambda i: (i % n_s, i // n_s)))
    return pl.pallas_call(
        functools.partial(_in_proj_kernel, apply_ln),
        grid=(m // tm,),
        in_specs=in_specs,
        out_specs=out_specs,
        out_shape=out_shape,
        compiler_params=pltpu.CompilerParams(
            dimension_semantics=("arbitrary",), vmem_limit_bytes=_vmem_limit(56 * 1024 * 1024)),
        name="in_proj_ln" if apply_ln else "in_proj",
    )(*args)


ATT_Q_PER_STEP = 4


def _att_kernel(q_ref, *refs):
    n_kv = ATT_Q_PER_STEP + 2
    k_refs, v_refs, (gen_ref, o_ref, bias_ref) = refs[:n_kv], refs[n_kv:2 * n_kv], refs[2 * n_kv:]
    tq = ATT_Q_BLOCK

    @pl.when((pl.program_id(0) == 0) & (pl.program_id(1) == 0))
    def _():
        qc = lax.broadcasted_iota(jnp.int32, (tq, 3 * tq), 0) // CHUNK
        kc = lax.broadcasted_iota(jnp.int32, (tq, 3 * tq), 1) // CHUNK
        in_band = (kc >= qc) & (kc <= qc + LEFT_CHUNKS)
        for h in range(ATT_HEADS):
            gen = jnp.broadcast_to(gen_ref[h:h + 1, :], (tq, ATT_BIAS_PERIOD))
            toep = pltpu.roll(gen, 0, 1, stride=1, stride_axis=0)
            bias_ref[h] = jnp.where(in_band, toep[:, :3 * tq], MASK_VALUE)

    k_all = jnp.concatenate([r[...] for r in k_refs], axis=0)
    v_all = jnp.concatenate([r[...] for r in v_refs], axis=0)
    lane = lax.broadcasted_iota(jnp.int32, (1, LANES), 1)
    first = (lane == 0).astype(BF16)
    row = lax.broadcasted_iota(jnp.int32, (3 * tq, 1), 0)
    q_one = jnp.broadcast_to(first, (tq, LANES))
    v_ones = jnp.ones((3 * tq, LANES), BF16)
    for sub in range(ATT_Q_PER_STEP):
        qb = pl.program_id(1) * ATT_Q_PER_STEP + sub
        q = q_ref[sub * tq:(sub + 1) * tq, :]
        k = k_all[sub * tq:(sub + 3) * tq, :]
        v = v_all[sub * tq:(sub + 3) * tq, :]
        pen = jnp.where(row < tq, jnp.where(qb >= 2, 0.0, MASK_VALUE),
                        jnp.where(row < 2 * tq, jnp.where(qb >= 1, 0.0, MASK_VALUE), 0.0))
        k_pen = pen.astype(BF16) * first
        for p in range(ATT_WIDTH // LANES):
            qp = q[:, p * LANES:(p + 1) * LANES]
            k_ext = jnp.concatenate([k[:, p * LANES:(p + 1) * LANES], k_pen], axis=1)
            v_ext = jnp.concatenate([v[:, p * LANES:(p + 1) * LANES], v_ones], axis=1)
            outs = []
            for hh in range(2):
                h = 2 * p + hh
                head_lanes = ((lane // ATT_HEAD_DIM) == hh).astype(BF16)
                q_ext = jnp.concatenate([qp * head_lanes, q_one], axis=1)
                s = lax.dot_general(q_ext, k_ext, (((1,), (1,)), ((), ())), preferred_element_type=F32)
                s = s + bias_ref[h]
                e = jnp.exp2(s - jnp.max(s, -1, keepdims=True))
                o = jnp.dot(e.astype(BF16), v_ext, preferred_element_type=F32)
                outs.append(o[:, :LANES] / o[:, LANES:])
            o_ref[sub * tq:(sub + 1) * tq, p * LANES:(p + 1) * LANES] = jnp.where(
                (lane // ATT_HEAD_DIM) == 0, outs[0], outs[1]).astype(BF16)


ATT_BIAS_PERIOD = 4 * ATT_Q_BLOCK


def _att_bias_generator(rel_bias):
    tq = ATT_Q_BLOCK
    rb = rel_bias.astype(F32) * LOG2E
    nh = rb.shape[0]
    fill = lambda col, n: jnp.broadcast_to(rb[:, col:col + 1], (nh, n))
    g = jnp.concatenate([fill(2 * MAX_REL, 2 * tq - MAX_REL), rb[:, ::-1],
                         fill(0, tq - MAX_REL - 1), fill(2 * MAX_REL, tq)], axis=1)
    assert g.shape[1] == ATT_BIAS_PERIOD
    return g


def _attention(att_qkv, bias_gen, batch, seq):
    m = att_qkv.shape[0]
    tq = ATT_Q_BLOCK
    n_sub = ATT_Q_PER_STEP
    assert LEFT_CHUNKS * CHUNK == 2 * tq and seq % (n_sub * tq) == 0
    nq = seq // tq
    n_steps = nq // n_sub
    qspec = pl.BlockSpec((n_sub * tq, ATT_WIDTH), lambda b, i: (b * n_steps + i, 0))

    def kv(col, off):
        return pl.BlockSpec((tq, ATT_WIDTH),
                            lambda b, i: (b * nq + jnp.maximum(i * n_sub - 2 + off, 0), col))

    n_kv = n_sub + 2
    return pl.pallas_call(
        _att_kernel,
        grid=(batch, n_steps),
        in_specs=([qspec] + [kv(1, j) for j in range(n_kv)] + [kv(2, j) for j in range(n_kv)]
                  + [_resident((ATT_HEADS, ATT_BIAS_PERIOD))]),
        out_specs=pl.BlockSpec((n_sub * tq, ATT_WIDTH), lambda b, i: (b * n_steps + i, 0)),
        out_shape=jax.ShapeDtypeStruct((m, ATT_WIDTH), BF16),
        scratch_shapes=[pltpu.VMEM((ATT_HEADS, tq, 3 * tq), F32)],
        compiler_params=pltpu.CompilerParams(
            dimension_semantics=("arbitrary", "arbitrary"),
            vmem_limit_bytes=_vmem_limit(40 * 1024 * 1024)),
        name="band_attention",
    )(*([att_qkv] * (1 + 2 * n_kv)), bias_gen)


def _ret_kernel(q_ref, k_ref, v_ref, g_ref, dm_ref, qdec_ref, kdec_ref, gblk_ref, o_ref, state_ref):
    @pl.when(pl.program_id(1) == 0)
    def _():
        state_ref[...] = jnp.zeros_like(state_ref)

    lane = lax.broadcasted_iota(jnp.int32, (1, LANES), 1)
    sub = lax.broadcasted_iota(jnp.int32, (LANES, 1), 0)
    for sb in range(RET_BATCH_PER_STEP):
        q = q_ref[sb]
        k = k_ref[sb]
        kd = k.astype(F32) * kdec_ref[...]
        for p in range(RET_QK_WIDTH // LANES):
            qp = q[:, p * LANES:(p + 1) * LANES]
            kp = k[:, p * LANES:(p + 1) * LANES]
            kd_t = kd[:, p * LANES:(p + 1) * LANES].T
            st = state_ref[sb, p]
            st_b = st.astype(BF16)
            new_st = st * gblk_ref[p]
            for hh in range(2):
                h = 2 * p + hh
                qm = qp * ((lane // RET_QK_DIM) == hh).astype(BF16)
                s = lax.dot_general(qm, kp, (((1,), (1,)), ((), ())), preferred_element_type=F32)
                sd = (s * dm_ref[h]).astype(BF16)
                vh = v_ref[sb, :, h * RET_V_DIM:(h + 1) * RET_V_DIM]
                inner = jnp.dot(sd, vh, preferred_element_type=F32)
                cross = jnp.dot(qm, st_b, preferred_element_type=F32) * qdec_ref[h]
                out = inner + cross
                mu = jnp.mean(out, -1, keepdims=True)
                oc = out - mu
                var = jnp.mean(oc * oc, -1, keepdims=True)
                nrm = oc * lax.rsqrt(var + LN_EPS)
                gate = g_ref[sb, :, h * RET_V_DIM:(h + 1) * RET_V_DIM].astype(F32)
                o_ref[sb, :, h * RET_V_DIM:(h + 1) * RET_V_DIM] = (nrm * gate).astype(BF16)
                kd_h = jnp.where((sub // RET_QK_DIM) == hh, kd_t, 0.0).astype(BF16)
                new_st = new_st + jnp.dot(kd_h, vh, preferred_element_type=F32)
            state_ref[sb, p] = new_st


def _ret_tables():
    t = RET_BLOCK
    log_g = jnp.log(1.0 - jnp.power(2.0, -5.0 - jnp.arange(RET_HEADS, dtype=F32)))
    pos = jnp.arange(t, dtype=F32)
    n, mm = pos[:, None], pos[None, :]
    cn, cm = jnp.floor(n / CHUNK), jnp.floor(mm / CHUNK)
    expo = jnp.where(cn == cm, jnp.abs(n - mm), n - mm)
    dm = jnp.where((cm <= cn)[None], jnp.exp(log_g[:, None, None] * jnp.where(cm <= cn, expo, 0.0)[None]), 0.0)
    qdec = jnp.exp(log_g[:, None] * (pos + 1.0)[None, :])
    qdec = jnp.broadcast_to(qdec[:, :, None], (RET_HEADS, t, RET_V_DIM))
    kdec = jnp.exp(log_g[:, None] * (t - 1.0 - pos)[None, :])
    kdec = jnp.repeat(kdec.T, RET_QK_DIM, axis=1)
    gblk = jnp.repeat(jnp.exp(log_g * t), RET_QK_DIM).reshape(RET_QK_WIDTH // LANES, LANES, 1)
    gblk = jnp.broadcast_to(gblk, (RET_QK_WIDTH // LANES, LANES, RET_V_DIM))
    return dm, qdec, kdec, gblk


def _retention(ret_qk, ret_v, ret_g, tables, batch, seq):
    m = ret_qk.shape[0]
    t = RET_BLOCK
    nsb = RET_BATCH_PER_STEP
    assert seq % t == 0 and t % CHUNK == 0 and batch % nsb == 0
    dm, qdec, kdec, gblk = tables
    n_pairs = RET_QK_WIDTH // LANES
    rows = lambda w, col: pl.BlockSpec((nsb, t, w), lambda b, i: (b, i, col))
    as3d = lambda a: a.reshape(batch, seq, a.shape[-1])
    out = pl.pallas_call(
        _ret_kernel,
        grid=(batch // nsb, seq // t),
        in_specs=[rows(RET_QK_WIDTH, 0), rows(RET_QK_WIDTH, 1), rows(RET_V_WIDTH, 0), rows(RET_V_WIDTH, 0),
                  _resident((RET_HEADS, t, t)), _resident((RET_HEADS, t, RET_V_DIM)),
                  _resident((t, RET_QK_WIDTH)), _resident((n_pairs, LANES, RET_V_DIM))],
        out_specs=rows(RET_V_WIDTH, 0),
        out_shape=jax.ShapeDtypeStruct((batch, seq, RET_V_WIDTH), BF16),
        scratch_shapes=[pltpu.VMEM((nsb, n_pairs, LANES, RET_V_DIM), F32)],
        compiler_params=pltpu.CompilerParams(
            dimension_semantics=("arbitrary", "arbitrary"),
            vmem_limit_bytes=_vmem_limit(40 * 1024 * 1024)),
        name="chunk_retention",
    )(as3d(ret_qk), as3d(ret_qk), as3d(ret_v), as3d(ret_g), dm, qdec, kdec, gblk)
    return out.reshape(m, RET_V_WIDTH)


def _s5_disc_kernel(lre_ref, lim_ref, lstep_ref, bre_ref, bim_ref, lbr_ref, lbi_ref, bbr_ref, bbi_ref):
    lre, lim = lre_ref[...], lim_ref[...]
    step = jnp.exp(lstep_ref[...])
    mag = jnp.exp(lre * step)
    ang = lim * step
    lbr = mag * jnp.cos(ang)
    lbi = mag * jnp.sin(ang)
    den = lre * lre + lim * lim
    nr = lbr - 1.0
    cr = (nr * lre + lbi * lim) / den
    ci = (lbi * lre - nr * lim) / den
    lbr_ref[...] = lbr
    lbi_ref[...] = lbi
    bre, bim = bre_ref[...], bim_ref[...]
    bbr_ref[...] = cr * bre - ci * bim
    bbi_ref[...] = cr * bim + ci * bre


def _s5_discretise(lam_re, lam_im, log_step, b_re, b_im):
    gp = SSM_GROUPS * SSM_STATE
    row = lambda a: a.reshape(1, gp).astype(F32)
    lstep = jnp.repeat(log_step.astype(F32), SSM_STATE).reshape(1, gp)
    b_t = lambda b: b.astype(F32).reshape(gp, SSM_GROUP).T
    full = lambda s: pl.BlockSpec(s, lambda: (0,) * len(s))
    return pl.pallas_call(
        _s5_disc_kernel,
        in_specs=[full((1, gp))] * 3 + [full((SSM_GROUP, gp))] * 2,
        out_specs=[full((1, gp))] * 2 + [full((SSM_GROUP, gp))] * 2,
        out_shape=[jax.ShapeDtypeStruct((1, gp), F32)] * 2 + [jax.ShapeDtypeStruct((SSM_GROUP, gp), F32)] * 2,
        name="s5_discretise",
    )(row(lam_re), row(lam_im), lstep, b_t(b_re), b_t(b_im))


def _s5_matrices(lbr, lbi, bbr, bbi, c_re, c_im, batch):
    nt, gl = SSM_LANE_TILES, LANES // SSM_GROUP
    eye = jnp.eye(gl, dtype=F32)

    def expand(bb):
        bb = bb.reshape(SSM_GROUP, nt, gl, SSM_STATE).transpose(1, 2, 0, 3)
        return jnp.einsum('tgip,gh->tgihp', bb, eye).reshape(nt, LANES, SSM_TILE_STATE)

    def contract(c):
        c = c.astype(F32).reshape(nt, gl, SSM_GROUP, SSM_STATE)
        return jnp.einsum('tgop,gh->tgpho', c, eye).reshape(nt, SSM_TILE_STATE, LANES)

    b_big = jnp.concatenate([expand(bbr), expand(bbi)], axis=-1).astype(BF16)
    c_big = jnp.concatenate([contract(c_re), -contract(c_im)], axis=1).astype(BF16)
    lam_r = jnp.broadcast_to(lbr, (batch, SSM_GROUPS * SSM_STATE))
    lam_i = jnp.broadcast_to(lbi, (batch, SSM_GROUPS * SSM_STATE))
    return b_big, c_big, lam_r, lam_i


def _s5_kernel(batch, u_ref, bbig_ref, cbig_ref, lr_ref, li_ref, dskip_ref, wglu_ref, y_ref,
               x_ref, st_ref, tb_ref):
    @pl.when(pl.program_id(0) == 0)
    def _():
        st_ref[...] = jnp.zeros_like(st_ref)

    ts = SSM_TILE_STATE
    tt = SSM_T_BLOCK
    blocks = range(SSM_BLOCKS_PER_STEP)
    tiles = range(SSM_LANE_TILES)
    for blk in blocks:
        for b in range(batch):
            for kt in tiles:
                c0 = b * SSM_WIDTH + kt * LANES
                tb_ref[blk, kt, pl.ds(b, tt, stride=batch), :] = (
                    u_ref[blk * tt:(blk + 1) * tt, c0:c0 + LANES].astype(F32))
    u_f32 = [[tb_ref[blk, kt] for kt in tiles] for blk in blocks]
    for blk in blocks:
        for kt in tiles:
            x_ref[blk, :, 2 * ts * kt:2 * ts * (kt + 1)] = jnp.dot(
                u_f32[blk][kt].astype(BF16), bbig_ref[kt], preferred_element_type=F32)

    for blk in blocks:
        for kt in tiles:
            re0, im0 = 2 * ts * kt, 2 * ts * kt + ts
            lr = lr_ref[:, kt * ts:(kt + 1) * ts]
            li = li_ref[:, kt * ts:(kt + 1) * ts]
            xr, xi = st_ref[:, re0:re0 + ts], st_ref[:, im0:im0 + ts]
            for t in range(tt):
                rows = slice(t * batch, (t + 1) * batch)
                xr, xi = (lr * xr - li * xi + x_ref[blk, rows, re0:re0 + ts],
                          lr * xi + li * xr + x_ref[blk, rows, im0:im0 + ts])
                x_ref[blk, rows, re0:re0 + ts] = xr
                x_ref[blk, rows, im0:im0 + ts] = xi
            st_ref[:, re0:re0 + ts] = xr
            st_ref[:, im0:im0 + ts] = xi

    for blk in blocks:
        ys = []
        for kt in tiles:
            xs = x_ref[blk, :, 2 * ts * kt:2 * ts * (kt + 1)].astype(BF16)
            ys.append(jnp.dot(xs, cbig_ref[kt], preferred_element_type=F32))
        y = jnp.concatenate(ys, axis=-1) + dskip_ref[...] * jnp.concatenate(u_f32[blk], axis=-1)
        y = _gelu(y)
        y = y * jax.nn.sigmoid(jnp.dot(y.astype(BF16), wglu_ref[...], preferred_element_type=F32))
        for kt in tiles:
            tb_ref[blk, kt] = y[:, kt * LANES:(kt + 1) * LANES]
        for b in range(batch):
            for kt in tiles:
                c0 = b * SSM_WIDTH + kt * LANES
                y_ref[blk * tt:(blk + 1) * tt, c0:c0 + LANES] = (
                    tb_ref[blk, kt, pl.ds(b, tt, stride=batch), :].astype(BF16))


def _s5(u_sb, mats, d_skip, w_glu, batch, seq):
    b_big, c_big, lam_r, lam_i = mats
    nblk = SSM_BLOCKS_PER_STEP
    rows = SSM_T_BLOCK * batch
    assert seq % (nblk * SSM_T_BLOCK) == 0 and batch % 8 == 0
    n_state = 2 * SSM_TILE_STATE * SSM_LANE_TILES
    blk = pl.BlockSpec((nblk * SSM_T_BLOCK, batch * SSM_WIDTH), lambda j: (j, 0))
    return pl.pallas_call(
        functools.partial(_s5_kernel, batch),
        grid=(seq // (nblk * SSM_T_BLOCK),),
        in_specs=[blk, _resident(b_big.shape), _resident(c_big.shape), _resident(lam_r.shape),
                  _resident(lam_i.shape), _resident((1, SSM_WIDTH)), _resident((SSM_WIDTH, SSM_WIDTH))],
        out_specs=blk,
        out_shape=jax.ShapeDtypeStruct((seq, batch * SSM_WIDTH), BF16),
        scratch_shapes=[pltpu.VMEM((nblk, rows, n_state), F32), pltpu.VMEM((batch, n_state), F32),
                        pltpu.VMEM((nblk, SSM_LANE_TILES, rows, LANES), F32)],
        compiler_params=pltpu.CompilerParams(
            dimension_semantics=("arbitrary",), vmem_limit_bytes=_vmem_limit(40 * 1024 * 1024)),
        name="s5_scan",
    )(u_sb, b_big, c_big, lam_r, lam_i, d_skip, w_glu)


MERGE_ROW_TILE = 2 * ROW_TILE


def _merge_kernel(att_ref, ret_ref, yc_ref, h_ref, wg_ref, wa_ref, wb_ref, wc_ref, wo_ref,
                  g_ref, b_ref, out_ref):
    d = D_MODEL
    for r0 in range(0, MERGE_ROW_TILE, ROW_TILE):
        rows = slice(r0, r0 + ROW_TILE)
        h = h_ref[rows, :]
        hb = h.astype(BF16)
        merged = None
        for i, (x_ref, w_ref) in enumerate(((att_ref, wa_ref), (ret_ref, wb_ref), (yc_ref, wc_ref))):
            gate = jax.nn.sigmoid(jnp.dot(hb, wg_ref[:, i * d:(i + 1) * d], preferred_element_type=F32))
            term = gate * jnp.dot(x_ref[rows, :], w_ref[...], preferred_element_type=F32)
            merged = term if merged is None else merged + term
        y = DEEPNORM_ALPHA * h + jnp.dot(merged.astype(BF16), wo_ref[...], preferred_element_type=F32)
        out_ref[rows, :] = _layer_norm(y, g_ref[...], b_ref[...])


def _merge(att_o, ret_o, yc_sb, h, w_gate, wa, wb, wc, wo, ln_g, ln_b, batch, seq):
    m = h.shape[0]
    tm = MERGE_ROW_TILE
    assert seq % tm == 0
    n_s = seq // tm
    row = lambda w: pl.BlockSpec((tm, w), lambda i: (i, 0))
    return pl.pallas_call(
        _merge_kernel,
        grid=(m // tm,),
        in_specs=[row(ATT_WIDTH), row(RET_V_WIDTH),
                  pl.BlockSpec((tm, SSM_WIDTH), lambda i: (i % n_s, i // n_s)),
                  row(D_MODEL), _resident(w_gate.shape),
                  _resident(wa.shape), _resident(wb.shape), _resident(wc.shape), _resident(wo.shape),
                  _resident((1, D_MODEL)), _resident((1, D_MODEL))],
        out_specs=row(D_MODEL),
        out_shape=jax.ShapeDtypeStruct((m, D_MODEL), F32),
        compiler_params=pltpu.CompilerParams(
            dimension_semantics=("arbitrary",), vmem_limit_bytes=_vmem_limit(54 * 1024 * 1024)),
        name="gated_merge",
    )(att_o, ret_o, yc_sb, h, w_gate, wa, wb, wc, wo, ln_g, ln_b)


FF_T_BLOCK = 64
FF_ROW_PARTS = 4
GELU_C1 = math.sqrt(2.0 / math.pi)
GELU_C2 = 0.044715 * GELU_C1


def _ffn_kernel(batch, h_ref, wup_ref, cw_ref, cb_ref, wdn_ref, g_ref, b_ref, out_ref,
                tb_ref, ext_ref, act_ref):
    tt = FF_T_BLOCK
    rows = tt * batch
    halo = 2 * batch
    n_lane_tiles = D_MODEL // LANES

    @pl.when(pl.program_id(0) == 0)
    def _():
        ext_ref[0:halo, :] = jnp.zeros((halo, 2 * D_FF), F32)

    @pl.when(pl.program_id(0) != 0)
    def _():
        ext_ref[0:halo, :] = ext_ref[rows:rows + halo, :]

    for b in range(batch):
        for c in range(n_lane_tiles):
            tb_ref[c, pl.ds(b, tt, stride=batch), :] = h_ref[b, :, c * LANES:(c + 1) * LANES]
    hb = jnp.concatenate([tb_ref[c] for c in range(n_lane_tiles)], axis=-1).astype(BF16)
    n_chunks = D_FF // FF_CHUNK

    def up_proj(j):
        for col in (j * FF_CHUNK, D_FF + j * FF_CHUNK):
            cols = slice(col, col + FF_CHUNK)
            ext_ref[halo:halo + rows, cols] = jnp.dot(hb, wup_ref[:, cols], preferred_element_type=F32)

    def conv(col):
        cols = slice(col, col + FF_CHUNK)
        w = cw_ref[:, cols]
        return (ext_ref[0:rows, cols] * w[0:1] + ext_ref[batch:batch + rows, cols] * w[1:2]
                + ext_ref[halo:halo + rows, cols] * w[2:3] + cb_ref[:, cols])

    def conv_gate(j):
        a = conv(j * FF_CHUNK)
        g_half = conv(D_FF + j * FF_CHUNK)
        t = jnp.tanh(a * (GELU_C1 + GELU_C2 * (a * a)))
        act_ref[:, j * FF_CHUNK:(j + 1) * FF_CHUNK] = ((a + a * t) * g_half).astype(BF16)

    up_proj(0)
    for j in range(n_chunks):
        if j + 1 < n_chunks:
            up_proj(j + 1)
        conv_gate(j)

    tp = tt // FF_ROW_PARTS
    nr = tp * batch
    for part in range(FF_ROW_PARTS):
        r0 = part * nr
        h = jnp.concatenate([tb_ref[c, r0:r0 + nr, :] for c in range(n_lane_tiles)], axis=-1)
        y = DEEPNORM_ALPHA * h + jnp.dot(act_ref[r0:r0 + nr, :], wdn_ref[...], preferred_element_type=F32)
        y = _layer_norm(y, g_ref[...], b_ref[...])
        for c in range(n_lane_tiles):
            tb_ref[c, r0:r0 + nr, :] = y[:, c * LANES:(c + 1) * LANES]
        for b in range(batch):
            for c in range(n_lane_tiles):
                out_ref[b, part * tp:(part + 1) * tp, c * LANES:(c + 1) * LANES] = (
                    tb_ref[c, pl.ds(r0 + b, tp, stride=batch), :])


def _ffn(h, w_up, conv_w, conv_b, w_down, ln_g, ln_b, batch, seq):
    tt = FF_T_BLOCK
    rows = tt * batch
    assert (2 * D_FF) % COL_CHUNK == 0 and D_FF % FF_CHUNK == 0 and seq % tt == 0 and batch % 8 == 0
    half_gate = jnp.concatenate([jnp.ones((D_FF,), F32), jnp.full((D_FF,), 0.5, F32)])
    blk = pl.BlockSpec((batch, tt, D_MODEL), lambda j: (0, j, 0))
    out = pl.pallas_call(
        functools.partial(_ffn_kernel, batch),
        grid=(seq // tt,),
        in_specs=[blk, _resident(w_up.shape), _resident(conv_w.shape), _resident((1, 2 * D_FF)),
                  _resident(w_down.shape), _resident((1, D_MODEL)), _resident((1, D_MODEL))],
        out_specs=blk,
        out_shape=jax.ShapeDtypeStruct((batch, seq, D_MODEL), F32),
        scratch_shapes=[pltpu.VMEM((D_MODEL // LANES, rows, LANES), F32),
                        pltpu.VMEM((rows + 2 * batch, 2 * D_FF), F32),
                        pltpu.VMEM((rows, D_FF), BF16)],
        compiler_params=pltpu.CompilerParams(
            dimension_semantics=("arbitrary",), vmem_limit_bytes=_vmem_limit(58 * 1024 * 1024)),
        name="conv_ffn",
    )(h.reshape(batch, seq, D_MODEL), w_up, conv_w * half_gate, conv_b * half_gate, w_down, ln_g, ln_b)
    return out.reshape(batch * seq, D_MODEL)


def _rotary_tables(seq):
    half = RET_QK_DIM // 2
    inv = ROPE_BASE ** (-jnp.arange(0, RET_QK_DIM, 2, dtype=F32) / RET_QK_DIM)
    ang = jnp.arange(seq, dtype=F32)[:, None] * inv[None, :]
    cos, sin = jnp.cos(ang), jnp.sin(ang)
    assert cos.shape[1] == half
    heads_per_tile = LANES // RET_QK_DIM
    cos_t = jnp.tile(jnp.concatenate([cos, cos], -1), (1, heads_per_tile))
    sin_t = jnp.tile(jnp.concatenate([-sin, sin], -1), (1, heads_per_tile))
    return cos_t, sin_t


def kernel(x, ln_in_g, ln_in_b, w_in, rel_bias, w_proj_a, w_proj_b, w_proj_c, lam_re, lam_im, log_step,
           b_re, b_im, c_re, c_im, d_skip, w_glu, w_o, ln1_g, ln1_b, w_up, conv_w, conv_b, w_down,
           ln2_g, ln2_b):
    batch, seq, d = x.shape
    assert d == D_MODEL and w_in.shape == (DEPTH, D_MODEL, IN_WIDTH)
    m = batch * seq
    vec = lambda a: a.reshape(1, -1).astype(F32)
    cos_t, sin_t = _rotary_tables(seq)
    ret_tabs = _ret_tables()
    h = x.reshape(m, d)
    for l in range(DEPTH):
        w_mix = w_in[l, :, :MIX_WIDTH].astype(BF16)
        w_gate = w_in[l, :, MIX_WIDTH:].astype(BF16)
        if l == 0:
            h, att_qkv, ret_qk, ret_v, ret_g, u_sb = _in_proj(
                h, vec(ln_in_g), vec(ln_in_b), w_mix, cos_t, sin_t, batch, seq)
        else:
            att_qkv, ret_qk, ret_v, ret_g, u_sb = _in_proj(
                h, None, None, w_mix, cos_t, sin_t, batch, seq)
        att_o = _attention(att_qkv, _att_bias_generator(rel_bias[l]), batch, seq)
        ret_o = _retention(ret_qk, ret_v, ret_g, ret_tabs, batch, seq)
        lbr, lbi, bbr, bbi = _s5_discretise(lam_re[l], lam_im[l], log_step[l], b_re[l], b_im[l])
        mats = _s5_matrices(lbr, lbi, bbr, bbi, c_re[l], c_im[l], batch)
        yc_sb = _s5(u_sb, mats, vec(d_skip[l]), w_glu[l].astype(BF16), batch, seq)
        h = _merge(att_o, ret_o, yc_sb, h, w_gate, w_proj_a[l].astype(BF16), w_proj_b[l].astype(BF16),
                   w_proj_c[l].astype(BF16), w_o[l].astype(BF16), vec(ln1_g[l]), vec(ln1_b[l]), batch, seq)
        h = _ffn(h, w_up[l].astype(BF16), conv_w[l].astype(F32), vec(conv_b[l]), w_down[l].astype(BF16),
                 vec(ln2_g[l]), vec(ln2_b[l]), batch, seq)
    return h.reshape(batch, seq, d)
```

```python
import functools
import math

import jax
import jax.numpy as jnp
from jax import lax
from jax.experimental import pallas as pl
from jax.experimental.pallas import tpu as pltpu

F32 = jnp.float32
BF16 = jnp.bfloat16

D_MODEL = 1024
DEPTH = 2
CHUNK = 64
ATT_HEADS = 8
ATT_HEAD_DIM = 64
ATT_WIDTH = ATT_HEADS * ATT_HEAD_DIM
LEFT_CHUNKS = 8
MAX_REL = 128
RET_HEADS = 8
RET_QK_DIM = 64
RET_V_DIM = 128
RET_QK_WIDTH = RET_HEADS * RET_QK_DIM
RET_V_WIDTH = RET_HEADS * RET_V_DIM
ROPE_BASE = 10000.0
SSM_WIDTH = 512
SSM_GROUP = 16
SSM_GROUPS = SSM_WIDTH // SSM_GROUP
SSM_STATE = 64
D_FF = 2816
N_BRANCH = 3
IN_WIDTH = 8192
MIX_WIDTH = IN_WIDTH - N_BRANCH * D_MODEL
DEEPNORM_ALPHA = (2.0 * DEPTH) ** 0.25
LN_EPS = 1e-5
MASK_VALUE = -1e30
LOG2E = math.log2(math.e)

LANES = 128
V7X_VMEM_BYTES = 64 * 1024 * 1024

ROW_TILE = 512
IN_ROW_TILE = 1024
COL_CHUNK = 512
ATT_Q_BLOCK = 256
RET_BLOCK = 256
RET_BATCH_PER_STEP = 4
SSM_T_BLOCK = 64
SSM_BLOCKS_PER_STEP = 4
SSM_LANE_TILES = SSM_WIDTH // LANES
SSM_TILE_STATE = (LANES // SSM_GROUP) * SSM_STATE
FF_CHUNK = 256


def _vmem_limit(nbytes):
    return int(min(nbytes, V7X_VMEM_BYTES - 6 * 1024 * 1024))


def _resident(shape):
    nd = len(shape)
    return pl.BlockSpec(shape, lambda *_: (0,) * nd, pipeline_mode=pl.Buffered(1))


def _layer_norm(x, g, b):
    mu = jnp.mean(x, -1, keepdims=True)
    xc = x - mu
    var = jnp.mean(xc * xc, -1, keepdims=True)
    return xc * lax.rsqrt(var + LN_EPS) * g + b


GELU_C1 = math.sqrt(2.0 / math.pi)
GELU_C2 = 0.044715 * GELU_C1


def _gelu_x2(x):
    return x + x * jnp.tanh(x * (GELU_C1 + GELU_C2 * (x * x)))


def _in_proj_kernel(apply_ln, *refs):
    if apply_ln:
        (x_ref, g_ref, b_ref, w_ref, cos_ref, sin_ref,
         hf_ref, att_ref, rqk_ref, rv_ref, rg_ref, u_ref) = refs
    else:
        (x_ref, w_ref, cos_ref, sin_ref,
         att_ref, rqk_ref, rv_ref, rg_ref, u_ref) = refs
    x = x_ref[...]
    if apply_ln:
        x = _layer_norm(x, g_ref[...], b_ref[...])
        hf_ref[...] = x
    xb = x.astype(BF16)

    def proj(c):
        return jnp.dot(xb, w_ref[:, c * COL_CHUNK:(c + 1) * COL_CHUNK], preferred_element_type=F32)

    lane = lax.broadcasted_iota(jnp.int32, (1, COL_CHUNK), 1)
    first_half = (lane % RET_QK_DIM) < (RET_QK_DIM // 2)
    half = RET_QK_DIM // 2

    cos_t = jnp.concatenate([cos_ref[...]] * (COL_CHUNK // LANES), axis=1)
    sin_t = jnp.concatenate([sin_ref[...]] * (COL_CHUNK // LANES), axis=1)

    def rotary(a):
        swapped = jnp.where(first_half, pltpu.roll(a, COL_CHUNK - half, 1), pltpu.roll(a, half, 1))
        return a * cos_t + swapped * sin_t

    att_ref[:, 0:512] = (proj(0) * (ATT_HEAD_DIM ** -0.5 * LOG2E)).astype(BF16)
    att_ref[:, 512:1024] = proj(1).astype(BF16)
    att_ref[:, 1024:1536] = proj(2).astype(BF16)
    rqk_ref[:, 0:512] = (rotary(proj(3)) * (RET_QK_DIM ** -0.5)).astype(BF16)
    rqk_ref[:, 512:1024] = rotary(proj(4)).astype(BF16)
    rv_ref[:, 0:512] = proj(5).astype(BF16)
    rv_ref[:, 512:1024] = proj(6).astype(BF16)
    for c in range(2):
        a = proj(7 + c)
        rg_ref[:, c * 512:(c + 1) * 512] = (a * jax.nn.sigmoid(a)).astype(BF16)
    u_ref[...] = proj(9).astype(BF16)


def _in_proj(x2d, ln_g, ln_b, w_bf16, cos_t, sin_t, batch, seq):
    m = x2d.shape[0]
    tm = IN_ROW_TILE
    assert seq % tm == 0 and w_bf16.shape == (D_MODEL, MIX_WIDTH)
    n_s = seq // tm
    apply_ln = ln_g is not None
    row = lambda w: pl.BlockSpec((tm, w), lambda i: (i, 0))
    in_specs = [row(D_MODEL)]
    args = [x2d]
    if apply_ln:
        in_specs += [_resident((1, D_MODEL)), _resident((1, D_MODEL))]
        args += [ln_g, ln_b]
    in_specs += [_resident((D_MODEL, MIX_WIDTH)),
                 pl.BlockSpec((tm, LANES), lambda i: (i % n_s, 0)),
                 pl.BlockSpec((tm, LANES), lambda i: (i % n_s, 0))]
    args += [w_bf16, cos_t, sin_t]
    out_shape, out_specs = [], []
    if apply_ln:
        out_shape.append(jax.ShapeDtypeStruct((m, D_MODEL), F32))
        out_specs.append(row(D_MODEL))
    for w in (3 * ATT_WIDTH, 2 * RET_QK_WIDTH, RET_V_WIDTH, RET_V_WIDTH):
        out_shape.append(jax.ShapeDtypeStruct((m, w), BF16))
        out_specs.append(row(w))
    out_shape.append(jax.ShapeDtypeStruct((seq, batch * SSM_WIDTH), BF16))
    out_specs.append(pl.BlockSpec((tm, SSM_WIDTH), lambda i: (i % n_s, i // n_s)))
    return pl.pallas_call(
        functools.partial(_in_proj_kernel, apply_ln),
        grid=(m // tm,),
        in_specs=in_specs,
        out_specs=out_specs,
        out_shape=out_shape,
        compiler_params=pltpu.CompilerParams(
            dimension_semantics=("arbitrary",), vmem_limit_bytes=_vmem_limit(56 * 1024 * 1024)),
        name="in_proj_ln" if apply_ln else "in_proj",
    )(*args)


ATT_Q_PER_STEP = 4


def _att_kernel(q_ref, *refs):
    n_kv = ATT_Q_PER_STEP + 2
    k_refs, v_refs, (gen_ref, o_ref, bias_ref) = refs[:n_kv], refs[n_kv:2 * n_kv], refs[2 * n_kv:]
    tq = ATT_Q_BLOCK

    @pl.when((pl.program_id(0) == 0) & (pl.program_id(1) == 0))
    def _():
        qc = lax.broadcasted_iota(jnp.int32, (tq, 3 * tq), 0) // CHUNK
        kc = lax.broadcasted_iota(jnp.int32, (tq, 3 * tq), 1) // CHUNK
        in_band = (kc >= qc) & (kc <= qc + LEFT_CHUNKS)
        for h in range(ATT_HEADS):
            gen = jnp.broadcast_to(gen_ref[h:h + 1, :], (tq, ATT_BIAS_PERIOD))
            toep = pltpu.roll(gen, 0, 1, stride=1, stride_axis=0)
            bias_ref[h] = jnp.where(in_band, toep[:, :3 * tq], MASK_VALUE)

    k_all = jnp.concatenate([r[...] for r in k_refs], axis=0)
    v_all = jnp.concatenate([r[...] for r in v_refs], axis=0)
    lane = lax.broadcasted_iota(jnp.int32, (1, LANES), 1)
    first = (lane == 0).astype(BF16)
    row = lax.broadcasted_iota(jnp.int32, (3 * tq, 1), 0)
    q_one = jnp.broadcast_to(first, (tq, LANES))
    v_ones = jnp.ones((3 * tq, LANES), BF16)
    for sub in range(ATT_Q_PER_STEP):
        qb = pl.program_id(1) * ATT_Q_PER_STEP + sub
        q = q_ref[sub * tq:(sub + 1) * tq, :]
        k = k_all[sub * tq:(sub + 3) * tq, :]
        v = v_all[sub * tq:(sub + 3) * tq, :]
        pen = jnp.where(row < tq, jnp.where(qb >= 2, 0.0, MASK_VALUE),
                        jnp.where(row < 2 * tq, jnp.where(qb >= 1, 0.0, MASK_VALUE), 0.0))
        k_pen = pen.astype(BF16) * first
        for p in range(ATT_WIDTH // LANES):
            qp = q[:, p * LANES:(p + 1) * LANES]
            k_ext = jnp.concatenate([k[:, p * LANES:(p + 1) * LANES], k_pen], axis=1)
            v_ext = jnp.concatenate([v[:, p * LANES:(p + 1) * LANES], v_ones], axis=1)
            outs = []
            for hh in range(2):
                h = 2 * p + hh
                head_lanes = ((lane // ATT_HEAD_DIM) == hh).astype(BF16)
                q_ext = jnp.concatenate([qp * head_lanes, q_one], axis=1)
                s = lax.dot_general(q_ext, k_ext, (((1,), (1,)), ((), ())), preferred_element_type=F32)
                s = s + bias_ref[h]
                e = jnp.exp2(s - jnp.max(s, -1, keepdims=True))
                o = jnp.dot(e.astype(BF16), v_ext, preferred_element_type=F32)
                outs.append(o[:, :LANES] / o[:, LANES:])
            o_ref[sub * tq:(sub + 1) * tq, p * LANES:(p + 1) * LANES] = jnp.where(
                (lane // ATT_HEAD_DIM) == 0, outs[0], outs[1]).astype(BF16)


ATT_BIAS_PERIOD = 4 * ATT_Q_BLOCK


def _att_bias_generator(rel_bias):
    tq = ATT_Q_BLOCK
    rb = rel_bias.astype(F32) * LOG2E
    nh = rb.shape[0]
    fill = lambda col, n: jnp.broadcast_to(rb[:, col:col + 1], (nh, n))
    g = jnp.concatenate([fill(2 * MAX_REL, 2 * tq - MAX_REL), rb[:, ::-1],
                         fill(0, tq - MAX_REL - 1), fill(2 * MAX_REL, tq)], axis=1)
    assert g.shape[1] == ATT_BIAS_PERIOD
    return g


def _attention(att_qkv, bias_gen, batch, seq):
    m = att_qkv.shape[0]
    tq = ATT_Q_BLOCK
    n_sub = ATT_Q_PER_STEP
    assert LEFT_CHUNKS * CHUNK == 2 * tq and seq % (n_sub * tq) == 0
    nq = seq // tq
    n_steps = nq // n_sub
    qspec = pl.BlockSpec((n_sub * tq, ATT_WIDTH), lambda b, i: (b * n_steps + i, 0))

    def kv(col, off):
        return pl.BlockSpec((tq, ATT_WIDTH),
                            lambda b, i: (b * nq + jnp.maximum(i * n_sub - 2 + off, 0), col))

    n_kv = n_sub + 2
    return pl.pallas_call(
        _att_kernel,
        grid=(batch, n_steps),
        in_specs=([qspec] + [kv(1, j) for j in range(n_kv)] + [kv(2, j) for j in range(n_kv)]
                  + [_resident((ATT_HEADS, ATT_BIAS_PERIOD))]),
        out_specs=pl.BlockSpec((n_sub * tq, ATT_WIDTH), lambda b, i: (b * n_steps + i, 0)),
        out_shape=jax.ShapeDtypeStruct((m, ATT_WIDTH), BF16),
        scratch_shapes=[pltpu.VMEM((ATT_HEADS, tq, 3 * tq), F32)],
        compiler_params=pltpu.CompilerParams(
            dimension_semantics=("arbitrary", "arbitrary"),
            vmem_limit_bytes=_vmem_limit(40 * 1024 * 1024)),
        name="band_attention",
    )(*([att_qkv] * (1 + 2 * n_kv)), bias_gen)


def _ret_kernel(q_ref, k_ref, v_ref, g_ref, dm_ref, qdec_ref, kdec_ref, gblk_ref, o_ref, state_ref):
    @pl.when(pl.program_id(1) == 0)
    def _():
        state_ref[...] = jnp.zeros_like(state_ref)

    lane = lax.broadcasted_iota(jnp.int32, (1, LANES), 1)
    sub = lax.broadcasted_iota(jnp.int32, (LANES, 1), 0)
    for sb in range(RET_BATCH_PER_STEP):
        q = q_ref[sb]
        k = k_ref[sb]
        kd = k.astype(F32) * kdec_ref[...]
        for p in range(RET_QK_WIDTH // LANES):
            qp = q[:, p * LANES:(p + 1) * LANES]
            kp = k[:, p * LANES:(p + 1) * LANES]
            kd_t = kd[:, p * LANES:(p + 1) * LANES].T
            st = state_ref[sb, p]
            st_b = st.astype(BF16)
            new_st = st * gblk_ref[p]
            for hh in range(2):
                h = 2 * p + hh
                qm = qp * ((lane // RET_QK_DIM) == hh).astype(BF16)
                s = lax.dot_general(qm, kp, (((1,), (1,)), ((), ())), preferred_element_type=F32)
                sd = (s * dm_ref[h]).astype(BF16)
                vh = v_ref[sb, :, h * RET_V_DIM:(h + 1) * RET_V_DIM]
                inner = jnp.dot(sd, vh, preferred_element_type=F32)
                cross = jnp.dot(qm, st_b, preferred_element_type=F32) * qdec_ref[h]
                out = inner + cross
                mu = jnp.mean(out, -1, keepdims=True)
                oc = out - mu
                var = jnp.mean(oc * oc, -1, keepdims=True)
                nrm = oc * lax.rsqrt(var + LN_EPS)
                gate = g_ref[sb, :, h * RET_V_DIM:(h + 1) * RET_V_DIM].astype(F32)
                o_ref[sb, :, h * RET_V_DIM:(h + 1) * RET_V_DIM] = (nrm * gate).astype(BF16)
                kd_h = jnp.where((sub // RET_QK_DIM) == hh, kd_t, 0.0).astype(BF16)
                new_st = new_st + jnp.dot(kd_h, vh, preferred_element_type=F32)
            state_ref[sb, p] = new_st


def _ret_tables():
    t = RET_BLOCK
    log_g = jnp.log(1.0 - jnp.power(2.0, -5.0 - jnp.arange(RET_HEADS, dtype=F32)))
    pos = jnp.arange(t, dtype=F32)
    n, mm = pos[:, None], pos[None, :]
    cn, cm = jnp.floor(n / CHUNK), jnp.floor(mm / CHUNK)
    expo = jnp.where(cn == cm, jnp.abs(n - mm), n - mm)
    dm = jnp.where((cm <= cn)[None], jnp.exp(log_g[:, None, None] * jnp.where(cm <= cn, expo, 0.0)[None]), 0.0)
    qdec = jnp.exp(log_g[:, None] * (pos + 1.0)[None, :])
    qdec = jnp.broadcast_to(qdec[:, :, None], (RET_HEADS, t, RET_V_DIM))
    kdec = jnp.exp(log_g[:, None] * (t - 1.0 - pos)[None, :])
    kdec = jnp.repeat(kdec.T, RET_QK_DIM, axis=1)
    gblk = jnp.repeat(jnp.exp(log_g * t), RET_QK_DIM).reshape(RET_QK_WIDTH // LANES, LANES, 1)
    gblk = jnp.broadcast_to(gblk, (RET_QK_WIDTH // LANES, LANES, RET_V_DIM))
    return dm, qdec, kdec, gblk


def _retention(ret_qk, ret_v, ret_g, tables, batch, seq):
    m = ret_qk.shape[0]
    t = RET_BLOCK
    nsb = RET_BATCH_PER_STEP
    assert seq % t == 0 and t % CHUNK == 0 and batch % nsb == 0
    dm, qdec, kdec, gblk = tables
    n_pairs = RET_QK_WIDTH // LANES
    rows = lambda w, col: pl.BlockSpec((nsb, t, w), lambda b, i: (b, i, col))
    as3d = lambda a: a.reshape(batch, seq, a.shape[-1])
    out = pl.pallas_call(
        _ret_kernel,
        grid=(batch // nsb, seq // t),
        in_specs=[rows(RET_QK_WIDTH, 0), rows(RET_QK_WIDTH, 1), rows(RET_V_WIDTH, 0), rows(RET_V_WIDTH, 0),
                  _resident((RET_HEADS, t, t)), _resident((RET_HEADS, t, RET_V_DIM)),
                  _resident((t, RET_QK_WIDTH)), _resident((n_pairs, LANES, RET_V_DIM))],
        out_specs=rows(RET_V_WIDTH, 0),
        out_shape=jax.ShapeDtypeStruct((batch, seq, RET_V_WIDTH), BF16),
        scratch_shapes=[pltpu.VMEM((nsb, n_pairs, LANES, RET_V_DIM), F32)],
        compiler_params=pltpu.CompilerParams(
            dimension_semantics=("arbitrary", "arbitrary"),
            vmem_limit_bytes=_vmem_limit(40 * 1024 * 1024)),
        name="chunk_retention",
    )(as3d(ret_qk), as3d(ret_qk), as3d(ret_v), as3d(ret_g), dm, qdec, kdec, gblk)
    return out.reshape(m, RET_V_WIDTH)


def _s5_disc_kernel(lre_ref, lim_ref, lstep_ref, bre_ref, bim_ref, lbr_ref, lbi_ref, bbr_ref, bbi_ref):
    lre, lim = lre_ref[...], lim_ref[...]
    step = jnp.exp(lstep_ref[...])
    mag = jnp.exp(lre * step)
    ang = lim * step
    lbr = mag * jnp.cos(ang)
    lbi = mag * jnp.sin(ang)
    den = lre * lre + lim * lim
    nr = lbr - 1.0
    cr = (nr * lre + lbi * lim) / den
    ci = (lbi * lre - nr * lim) / den
    lbr_ref[...] = lbr
    lbi_ref[...] = lbi
    bre, bim = bre_ref[...], bim_ref[...]
    bbr_ref[...] = cr * bre - ci * bim
    bbi_ref[...] = cr * bim + ci * bre


def _s5_discretise(lam_re, lam_im, log_step, b_re, b_im):
    gp = SSM_GROUPS * SSM_STATE
    row = lambda a: a.reshape(1, gp).astype(F32)
    lstep = jnp.repeat(log_step.astype(F32), SSM_STATE).reshape(1, gp)
    b_t = lambda b: b.astype(F32).reshape(gp, SSM_GROUP).T
    full = lambda s: pl.BlockSpec(s, lambda: (0,) * len(s))
    return pl.pallas_call(
        _s5_disc_kernel,
        in_specs=[full((1, gp))] * 3 + [full((SSM_GROUP, gp))] * 2,
        out_specs=[full((1, gp))] * 2 + [full((SSM_GROUP, gp))] * 2,
        out_shape=[jax.ShapeDtypeStruct((1, gp), F32)] * 2 + [jax.ShapeDtypeStruct((SSM_GROUP, gp), F32)] * 2,
        name="s5_discretise",
    )(row(lam_re), row(lam_im), lstep, b_t(b_re), b_t(b_im))


def _s5_matrices(lbr, lbi, bbr, bbi, c_re, c_im, batch):
    nt, gl = SSM_LANE_TILES, LANES // SSM_GROUP
    eye = jnp.eye(gl, dtype=F32)

    def expand(bb):
        bb = bb.reshape(SSM_GROUP, nt, gl, SSM_STATE).transpose(1, 2, 0, 3)
        return jnp.einsum('tgip,gh->tgihp', bb, eye).reshape(nt, LANES, SSM_TILE_STATE)

    def contract(c):
        c = c.astype(F32).reshape(nt, gl, SSM_GROUP, SSM_STATE)
        return jnp.einsum('tgop,gh->tgpho', c, eye).reshape(nt, SSM_TILE_STATE, LANES)

    b_big = jnp.concatenate([expand(bbr), expand(bbi)], axis=-1).astype(BF16)
    c_big = jnp.concatenate([contract(c_re), -contract(c_im)], axis=1).astype(BF16)
    lam_r = jnp.broadcast_to(lbr, (batch, SSM_GROUPS * SSM_STATE))
    lam_i = jnp.broadcast_to(lbi, (batch, SSM_GROUPS * SSM_STATE))
    return b_big, c_big, lam_r, lam_i


def _s5_kernel(batch, u_ref, bbig_ref, cbig_ref, lr_ref, li_ref, dskip_ref, wglu_ref, y_ref,
               x_ref, st_ref, tb_ref):
    @pl.when(pl.program_id(0) == 0)
    def _():
        st_ref[...] = jnp.zeros_like(st_ref)

    ts = SSM_TILE_STATE
    tt = SSM_T_BLOCK
    blocks = range(SSM_BLOCKS_PER_STEP)
    tiles = range(SSM_LANE_TILES)
    for blk in blocks:
        for b in range(batch):
            for kt in tiles:
                c0 = b * SSM_WIDTH + kt * LANES
                tb_ref[blk, kt, pl.ds(b, tt, stride=batch), :] = (
                    u_ref[blk * tt:(blk + 1) * tt, c0:c0 + LANES].astype(F32))
    u_f32 = [[tb_ref[blk, kt] for kt in tiles] for blk in blocks]
    for blk in blocks:
        for kt in tiles:
            x_ref[blk, :, 2 * ts * kt:2 * ts * (kt + 1)] = jnp.dot(
                u_f32[blk][kt].astype(BF16), bbig_ref[kt], preferred_element_type=F32)

    for blk in blocks:
        for kt in tiles:
            re0, im0 = 2 * ts * kt, 2 * ts * kt + ts
            lr = lr_ref[:, kt * ts:(kt + 1) * ts]
            li = li_ref[:, kt * ts:(kt + 1) * ts]
            xr, xi = st_ref[:, re0:re0 + ts], st_ref[:, im0:im0 + ts]
            for t in range(tt):
                rows = slice(t * batch, (t + 1) * batch)
                xr, xi = (lr * xr - li * xi + x_ref[blk, rows, re0:re0 + ts],
                          lr * xi + li * xr + x_ref[blk, rows, im0:im0 + ts])
                x_ref[blk, rows, re0:re0 + ts] = xr
                x_ref[blk, rows, im0:im0 + ts] = xi
            st_ref[:, re0:re0 + ts] = xr
            st_ref[:, im0:im0 + ts] = xi

    for blk in blocks:
        ys = []
        for kt in tiles:
            xs = x_ref[blk, :, 2 * ts * kt:2 * ts * (kt + 1)].astype(BF16)
            ys.append(jnp.dot(xs, cbig_ref[kt], preferred_element_type=F32))
        y = jnp.concatenate(ys, axis=-1) + dskip_ref[...] * jnp.concatenate(u_f32[blk], axis=-1)
        z = _gelu_x2(y)
        y = (0.5 * z) * jax.nn.sigmoid(jnp.dot(z.astype(BF16), wglu_ref[...], preferred_element_type=F32))
        for kt in tiles:
            tb_ref[blk, kt] = y[:, kt * LANES:(kt + 1) * LANES]
        for b in range(batch):
            for kt in tiles:
                c0 = b * SSM_WIDTH + kt * LANES
                y_ref[blk * tt:(blk + 1) * tt, c0:c0 + LANES] = (
                    tb_ref[blk, kt, pl.ds(b, tt, stride=batch), :].astype(BF16))


def _s5(u_sb, mats, d_skip, w_glu, batch, seq):
    b_big, c_big, lam_r, lam_i = mats
    nblk = SSM_BLOCKS_PER_STEP
    rows = SSM_T_BLOCK * batch
    assert seq % (nblk * SSM_T_BLOCK) == 0 and batch % 8 == 0
    n_state = 2 * SSM_TILE_STATE * SSM_LANE_TILES
    blk = pl.BlockSpec((nblk * SSM_T_BLOCK, batch * SSM_WIDTH), lambda j: (j, 0))
    return pl.pallas_call(
        functools.partial(_s5_kernel, batch),
        grid=(seq // (nblk * SSM_T_BLOCK),),
        in_specs=[blk, _resident(b_big.shape), _resident(c_big.shape), _resident(lam_r.shape),
                  _resident(lam_i.shape), _resident((1, SSM_WIDTH)), _resident((SSM_WIDTH, SSM_WIDTH))],
        out_specs=blk,
        out_shape=jax.ShapeDtypeStruct((seq, batch * SSM_WIDTH), BF16),
        scratch_shapes=[pltpu.VMEM((nblk, rows, n_state), F32), pltpu.VMEM((batch, n_state), F32),
                        pltpu.VMEM((nblk, SSM_LANE_TILES, rows, LANES), F32)],
        compiler_params=pltpu.CompilerParams(
            dimension_semantics=("arbitrary",), vmem_limit_bytes=_vmem_limit(56 * 1024 * 1024)),
        name="s5_scan",
    )(u_sb, b_big, c_big, lam_r, lam_i, d_skip, w_glu)


MERGE_ROW_TILE = 2 * ROW_TILE


def _merge_kernel(att_ref, ret_ref, yc_ref, h_ref, wg_ref, wa_ref, wb_ref, wc_ref, wo_ref,
                  g_ref, b_ref, out_ref):
    d = D_MODEL
    for r0 in range(0, MERGE_ROW_TILE, ROW_TILE):
        rows = slice(r0, r0 + ROW_TILE)
        h = h_ref[rows, :]
        hb = h.astype(BF16)
        merged = None
        for i, (x_ref, w_ref) in enumerate(((att_ref, wa_ref), (ret_ref, wb_ref), (yc_ref, wc_ref))):
            gate = jax.nn.sigmoid(jnp.dot(hb, wg_ref[:, i * d:(i + 1) * d], preferred_element_type=F32))
            term = gate * jnp.dot(x_ref[rows, :], w_ref[...], preferred_element_type=F32)
            merged = term if merged is None else merged + term
        y = DEEPNORM_ALPHA * h + jnp.dot(merged.astype(BF16), wo_ref[...], preferred_element_type=F32)
        out_ref[rows, :] = _layer_norm(y, g_ref[...], b_ref[...])


def _merge(att_o, ret_o, yc_sb, h, w_gate, wa, wb, wc, wo, ln_g, ln_b, batch, seq):
    m = h.shape[0]
    tm = MERGE_ROW_TILE
    assert seq % tm == 0
    n_s = seq // tm
    row = lambda w: pl.BlockSpec((tm, w), lambda i: (i, 0))
    return pl.pallas_call(
        _merge_kernel,
        grid=(m // tm,),
        in_specs=[row(ATT_WIDTH), row(RET_V_WIDTH),
                  pl.BlockSpec((tm, SSM_WIDTH), lambda i: (i % n_s, i // n_s)),
                  row(D_MODEL), _resident(w_gate.shape),
                  _resident(wa.shape), _resident(wb.shape), _resident(wc.shape), _resident(wo.shape),
                  _resident((1, D_MODEL)), _resident((1, D_MODEL))],
        out_specs=row(D_MODEL),
        out_shape=jax.ShapeDtypeStruct((m, D_MODEL), F32),
        compiler_params=pltpu.CompilerParams(
            dimension_semantics=("arbitrary",), vmem_limit_bytes=_vmem_limit(54 * 1024 * 1024)),
        name="gated_merge",
    )(att_o, ret_o, yc_sb, h, w_gate, wa, wb, wc, wo, ln_g, ln_b)


FF_T_BLOCK = 128
FF_ROW_PARTS = 4


def _ffn_kernel(batch, h_ref, wup_ref, cw_ref, cb_ref, wdn_ref, g_ref, b_ref, out_ref,
                tb_ref, halo_ref, chunk_ref, act_ref):
    tt = FF_T_BLOCK
    rows = tt * batch
    halo = 2 * batch
    n_lane_tiles = D_MODEL // LANES

    @pl.when(pl.program_id(0) == 0)
    def _():
        halo_ref[...] = jnp.zeros_like(halo_ref)

    for b in range(batch):
        for c in range(n_lane_tiles):
            tb_ref[c, pl.ds(b, tt, stride=batch), :] = h_ref[b, :, c * LANES:(c + 1) * LANES]
    hb = jnp.concatenate([tb_ref[c] for c in range(n_lane_tiles)], axis=-1).astype(BF16)

    def conv(buf, col):
        cols = slice(col, col + FF_CHUNK)
        buf[0:halo, :] = halo_ref[:, cols]
        buf[halo:halo + rows, :] = jnp.dot(hb, wup_ref[:, cols], preferred_element_type=F32)
        halo_ref[:, cols] = buf[rows:rows + halo, :]
        w = cw_ref[:, cols]
        return (buf[0:rows, :] * w[0:1] + buf[batch:batch + rows, :] * w[1:2]
                + buf[halo:halo + rows, :] * w[2:3] + cb_ref[:, cols])

    for j in range(D_FF // FF_CHUNK):
        a = conv(chunk_ref.at[j % 2, 0], j * FF_CHUNK)
        g_half = conv(chunk_ref.at[j % 2, 1], D_FF + j * FF_CHUNK)
        act_ref[:, j * FF_CHUNK:(j + 1) * FF_CHUNK] = (_gelu_x2(a) * g_half).astype(BF16)

    tp = tt // FF_ROW_PARTS
    nr = tp * batch
    for part in range(FF_ROW_PARTS):
        r0 = part * nr
        h = jnp.concatenate([tb_ref[c, r0:r0 + nr, :] for c in range(n_lane_tiles)], axis=-1)
        y = DEEPNORM_ALPHA * h + jnp.dot(act_ref[r0:r0 + nr, :], wdn_ref[...], preferred_element_type=F32)
        y = _layer_norm(y, g_ref[...], b_ref[...])
        for c in range(n_lane_tiles):
            tb_ref[c, r0:r0 + nr, :] = y[:, c * LANES:(c + 1) * LANES]
        for b in range(batch):
            for c in range(n_lane_tiles):
                out_ref[b, part * tp:(part + 1) * tp, c * LANES:(c + 1) * LANES] = (
                    tb_ref[c, pl.ds(r0 + b, tp, stride=batch), :])


def _ffn(h, w_up, conv_w, conv_b, w_down, ln_g, ln_b, batch, seq):
    tt = FF_T_BLOCK
    rows = tt * batch
    assert D_FF % FF_CHUNK == 0 and tt % FF_ROW_PARTS == 0 and seq % tt == 0 and batch % 8 == 0
    half_gate = jnp.concatenate([jnp.ones((D_FF,), F32), jnp.full((D_FF,), 0.5, F32)])
    blk = pl.BlockSpec((batch, tt, D_MODEL), lambda j: (0, j, 0))
    out = pl.pallas_call(
        functools.partial(_ffn_kernel, batch),
        grid=(seq // tt,),
        in_specs=[blk, _resident(w_up.shape), _resident(conv_w.shape), _resident((1, 2 * D_FF)),
                  _resident(w_down.shape), _resident((1, D_MODEL)), _resident((1, D_MODEL))],
        out_specs=blk,
        out_shape=jax.ShapeDtypeStruct((batch, seq, D_MODEL), F32),
        scratch_shapes=[pltpu.VMEM((D_MODEL // LANES, rows, LANES), F32),
                        pltpu.VMEM((2 * batch, 2 * D_FF), F32),
                        pltpu.VMEM((2, 2, rows + 2 * batch, FF_CHUNK), F32),
                        pltpu.VMEM((rows, D_FF), BF16)],
        compiler_params=pltpu.CompilerParams(
            dimension_semantics=("arbitrary",), vmem_limit_bytes=_vmem_limit(58 * 1024 * 1024)),
        name="conv_ffn",
    )(h.reshape(batch, seq, D_MODEL), w_up, conv_w * half_gate, conv_b * half_gate, w_down, ln_g, ln_b)
    return out.reshape(batch * seq, D_MODEL)


def _rotary_tables(seq):
    half = RET_QK_DIM // 2
    inv = ROPE_BASE ** (-jnp.arange(0, RET_QK_DIM, 2, dtype=F32) / RET_QK_DIM)
    ang = jnp.arange(seq, dtype=F32)[:, None] * inv[None, :]
    cos, sin = jnp.cos(ang), jnp.sin(ang)
    assert cos.shape[1] == half
    heads_per_tile = LANES // RET_QK_DIM
    cos_t = jnp.tile(jnp.concatenate([cos, cos], -1), (1, heads_per_tile))
    sin_t = jnp.tile(jnp.concatenate([-sin, sin], -1), (1, heads_per_tile))
    return cos_t, sin_t


def kernel(x, ln_in_g, ln_in_b, w_in, rel_bias, w_proj_a, w_proj_b, w_proj_c, lam_re, lam_im, log_step,
           b_re, b_im, c_re, c_im, d_skip, w_glu, w_o, ln1_g, ln1_b, w_up, conv_w, conv_b, w_down,
           ln2_g, ln2_b):
    batch, seq, d = x.shape
    assert d == D_MODEL and w_in.shape == (DEPTH, D_MODEL, IN_WIDTH)
    m = batch * seq
    vec = lambda a: a.reshape(1, -1).astype(F32)
    cos_t, sin_t = _rotary_tables(seq)
    ret_tabs = _ret_tables()
    h = x.reshape(m, d)
    for l in range(DEPTH):
        w_mix = w_in[l, :, :MIX_WIDTH].astype(BF16)
        w_gate = w_in[l, :, MIX_WIDTH:].astype(BF16)
        if l == 0:
            h, att_qkv, ret_qk, ret_v, ret_g, u_sb = _in_proj(
                h, vec(ln_in_g), vec(ln_in_b), w_mix, cos_t, sin_t, batch, seq)
        else:
            att_qkv, ret_qk, ret_v, ret_g, u_sb = _in_proj(
                h, None, None, w_mix, cos_t, sin_t, batch, seq)
        att_o = _attention(att_qkv, _att_bias_generator(rel_bias[l]), batch, seq)
        ret_o = _retention(ret_qk, ret_v, ret_g, ret_tabs, batch, seq)
        lbr, lbi, bbr, bbi = _s5_discretise(lam_re[l], lam_im[l], log_step[l], b_re[l], b_im[l])
        mats = _s5_matrices(lbr, lbi, bbr, bbi, c_re[l], c_im[l], batch)
        yc_sb = _s5(u_sb, mats, vec(d_skip[l]), (0.5 * w_glu[l]).astype(BF16), batch, seq)
        h = _merge(att_o, ret_o, yc_sb, h, w_gate, w_proj_a[l].astype(BF16), w_proj_b[l].astype(BF16),
                   w_proj_c[l].astype(BF16), w_o[l].astype(BF16), vec(ln1_g[l]), vec(ln1_b[l]), batch, seq)
        h = _ffn(h, w_up[l].astype(BF16), conv_w[l].astype(F32), vec(conv_b[l]), w_down[l].astype(BF16),
                 vec(ln2_g[l]), vec(ln2_b[l]), batch, seq)
    return h.reshape(batch, seq, d)
```

```python
import functools
import math

import jax
import jax.numpy as jnp
from jax import lax
from jax.experimental import pallas as pl
from jax.experimental.pallas import tpu as pltpu

F32 = jnp.float32
BF16 = jnp.bfloat16

D_MODEL = 1024
DEPTH = 2
CHUNK = 64
ATT_HEADS = 8
ATT_HEAD_DIM = 64
ATT_WIDTH = ATT_HEADS * ATT_HEAD_DIM
LEFT_CHUNKS = 8
MAX_REL = 128
RET_HEADS = 8
RET_QK_DIM = 64
RET_V_DIM = 128
RET_QK_WIDTH = RET_HEADS * RET_QK_DIM
RET_V_WIDTH = RET_HEADS * RET_V_DIM
ROPE_BASE = 10000.0
SSM_WIDTH = 512
SSM_GROUP = 16
SSM_GROUPS = SSM_WIDTH // SSM_GROUP
SSM_STATE = 64
D_FF = 2816
N_BRANCH = 3
IN_WIDTH = 8192
MIX_WIDTH = IN_WIDTH - N_BRANCH * D_MODEL
DEEPNORM_ALPHA = (2.0 * DEPTH) ** 0.25
LN_EPS = 1e-5
MASK_VALUE = -1e30
LOG2E = math.log2(math.e)

LANES = 128
V7X_VMEM_BYTES = 64 * 1024 * 1024

ROW_TILE = 512
IN_ROW_TILE = 1024
COL_CHUNK = 512
ATT_Q_BLOCK = 256
RET_BLOCK = 256
RET_BATCH_PER_STEP = 4
SSM_T_BLOCK = 64
SSM_BLOCKS_PER_STEP = 4
SSM_LANE_TILES = SSM_WIDTH // LANES
SSM_TILE_STATE = (LANES // SSM_GROUP) * SSM_STATE
FF_CHUNK = 256


def _vmem_limit(nbytes):
    return int(min(nbytes, V7X_VMEM_BYTES - 6 * 1024 * 1024))


def _resident(shape):
    nd = len(shape)
    return pl.BlockSpec(shape, lambda *_: (0,) * nd, pipeline_mode=pl.Buffered(1))


def _layer_norm(x, g, b):
    mu = jnp.mean(x, -1, keepdims=True)
    xc = x - mu
    var = jnp.mean(xc * xc, -1, keepdims=True)
    return xc * lax.rsqrt(var + LN_EPS) * g + b


GELU_C1 = math.sqrt(2.0 / math.pi)
GELU_C2 = 0.044715 * GELU_C1


def _gelu_x2(x):
    return x + x * jnp.tanh(x * (GELU_C1 + GELU_C2 * (x * x)))


def _in_proj_kernel(apply_ln, *refs):
    if apply_ln:
        (x_ref, g_ref, b_ref, w_ref, cos_ref, sin_ref,
         hf_ref, att_ref, rqk_ref, rv_ref, rg_ref, u_ref) = refs
    else:
        (x_ref, w_ref, cos_ref, sin_ref,
         att_ref, rqk_ref, rv_ref, rg_ref, u_ref) = refs
    x = x_ref[...]
    if apply_ln:
        x = _layer_norm(x, g_ref[...], b_ref[...])
        hf_ref[...] = x
    xb = x.astype(BF16)

    def proj(c):
        return jnp.dot(xb, w_ref[:, c * COL_CHUNK:(c + 1) * COL_CHUNK], preferred_element_type=F32)

    lane = lax.broadcasted_iota(jnp.int32, (1, COL_CHUNK), 1)
    first_half = (lane % RET_QK_DIM) < (RET_QK_DIM // 2)
    half = RET_QK_DIM // 2

    cos_t = jnp.concatenate([cos_ref[...]] * (COL_CHUNK // LANES), axis=1)
    sin_t = jnp.concatenate([sin_ref[...]] * (COL_CHUNK // LANES), axis=1)

    def rotary(a):
        swapped = jnp.where(first_half, pltpu.roll(a, COL_CHUNK - half, 1), pltpu.roll(a, half, 1))
        return a * cos_t + swapped * sin_t

    att_ref[:, 0:512] = (proj(0) * (ATT_HEAD_DIM ** -0.5 * LOG2E)).astype(BF16)
    att_ref[:, 512:1024] = proj(1).astype(BF16)
    att_ref[:, 1024:1536] = proj(2).astype(BF16)
    rqk_ref[:, 0:512] = (rotary(proj(3)) * (RET_QK_DIM ** -0.5)).astype(BF16)
    rqk_ref[:, 512:1024] = rotary(proj(4)).astype(BF16)
    rv_ref[:, 0:512] = proj(5).astype(BF16)
    rv_ref[:, 512:1024] = proj(6).astype(BF16)
    for c in range(2):
        a = proj(7 + c)
        rg_ref[:, c * 512:(c + 1) * 512] = (a * jax.nn.sigmoid(a)).astype(BF16)
    u_ref[...] = proj(9).astype(BF16)


def _in_proj(x2d, ln_g, ln_b, w_bf16, cos_t, sin_t, batch, seq):
    m = x2d.shape[0]
    tm = IN_ROW_TILE
    assert seq % tm == 0 and w_bf16.shape == (D_MODEL, MIX_WIDTH)
    n_s = seq // tm
    apply_ln = ln_g is not None
    row = lambda w: pl.BlockSpec((tm, w), lambda i: (i, 0))
    in_specs = [row(D_MODEL)]
    args = [x2d]
    if apply_ln:
        in_specs += [_resident((1, D_MODEL)), _resident((1, D_MODEL))]
        args += [ln_g, ln_b]
    in_specs += [_resident((D_MODEL, MIX_WIDTH)),
                 pl.BlockSpec((tm, LANES), lambda i: (i % n_s, 0)),
                 pl.BlockSpec((tm, LANES), lambda i: (i % n_s, 0))]
    args += [w_bf16, cos_t, sin_t]
    out_shape, out_specs = [], []
    if apply_ln:
        out_shape.append(jax.ShapeDtypeStruct((m, D_MODEL), F32))
        out_specs.append(row(D_MODEL))
    for w in (3 * ATT_WIDTH, 2 * RET_QK_WIDTH, RET_V_WIDTH, RET_V_WIDTH):
        out_shape.append(jax.ShapeDtypeStruct((m, w), BF16))
        out_specs.append(row(w))
    out_shape.append(jax.ShapeDtypeStruct((seq, batch * SSM_WIDTH), BF16))
    out_specs.append(pl.BlockSpec((tm, SSM_WIDTH), lambda i: (i % n_s, i // n_s)))
    return pl.pallas_call(
        functools.partial(_in_proj_kernel, apply_ln),
        grid=(m // tm,),
        in_specs=in_specs,
        out_specs=out_specs,
        out_shape=out_shape,
        compiler_params=pltpu.CompilerParams(
            dimension_semantics=("arbitrary",), vmem_limit_bytes=_vmem_limit(56 * 1024 * 1024)),
        name="in_proj_ln" if apply_ln else "in_proj",
    )(*args)


ATT_Q_PER_STEP = 8


def _att_kernel(q_ref, *refs):
    n_kv = ATT_Q_PER_STEP + 2
    k_refs, v_refs, (gen_ref, o_ref, bias_ref) = refs[:n_kv], refs[n_kv:2 * n_kv], refs[2 * n_kv:]
    tq = ATT_Q_BLOCK

    @pl.when((pl.program_id(0) == 0) & (pl.program_id(1) == 0))
    def _():
        qc = lax.broadcasted_iota(jnp.int32, (tq, 3 * tq), 0) // CHUNK
        kc = lax.broadcasted_iota(jnp.int32, (tq, 3 * tq), 1) // CHUNK
        in_band = (kc >= qc) & (kc <= qc + LEFT_CHUNKS)
        for h in range(ATT_HEADS):
            gen = jnp.broadcast_to(gen_ref[h:h + 1, :], (tq, ATT_BIAS_PERIOD))
            toep = pltpu.roll(gen, 0, 1, stride=1, stride_axis=0)
            bias_ref[h] = jnp.where(in_band, toep[:, :3 * tq], MASK_VALUE)

    k_all = jnp.concatenate([r[...] for r in k_refs], axis=0)
    v_all = jnp.concatenate([r[...] for r in v_refs], axis=0)
    lane = lax.broadcasted_iota(jnp.int32, (1, LANES), 1)
    first = (lane == 0).astype(BF16)
    row = lax.broadcasted_iota(jnp.int32, (3 * tq, 1), 0)
    q_one = jnp.broadcast_to(first, (tq, LANES))
    v_ones = jnp.ones((3 * tq, LANES), BF16)
    for sub in range(ATT_Q_PER_STEP):
        qb = pl.program_id(1) * ATT_Q_PER_STEP + sub
        q = q_ref[sub * tq:(sub + 1) * tq, :]
        k = k_all[sub * tq:(sub + 3) * tq, :]
        v = v_all[sub * tq:(sub + 3) * tq, :]
        pen = jnp.where(row < tq, jnp.where(qb >= 2, 0.0, MASK_VALUE),
                        jnp.where(row < 2 * tq, jnp.where(qb >= 1, 0.0, MASK_VALUE), 0.0))
        k_pen = pen.astype(BF16) * first
        for p in range(ATT_WIDTH // LANES):
            qp = q[:, p * LANES:(p + 1) * LANES]
            k_ext = jnp.concatenate([k[:, p * LANES:(p + 1) * LANES], k_pen], axis=1)
            v_ext = jnp.concatenate([v[:, p * LANES:(p + 1) * LANES], v_ones], axis=1)
            outs = []
            for hh in range(2):
                h = 2 * p + hh
                head_lanes = ((lane // ATT_HEAD_DIM) == hh).astype(BF16)
                q_ext = jnp.concatenate([qp * head_lanes, q_one], axis=1)
                s = lax.dot_general(q_ext, k_ext, (((1,), (1,)), ((), ())), preferred_element_type=F32)
                s = s + bias_ref[h]
                e = jnp.exp2(s - jnp.max(s, -1, keepdims=True))
                o = jnp.dot(e.astype(BF16), v_ext, preferred_element_type=F32)
                outs.append(o[:, :LANES] / o[:, LANES:])
            o_ref[sub * tq:(sub + 1) * tq, p * LANES:(p + 1) * LANES] = jnp.where(
                (lane // ATT_HEAD_DIM) == 0, outs[0], outs[1]).astype(BF16)


ATT_BIAS_PERIOD = 4 * ATT_Q_BLOCK


def _att_bias_generator(rel_bias):
    tq = ATT_Q_BLOCK
    rb = rel_bias.astype(F32) * LOG2E
    nh = rb.shape[0]
    fill = lambda col, n: jnp.broadcast_to(rb[:, col:col + 1], (nh, n))
    g = jnp.concatenate([fill(2 * MAX_REL, 2 * tq - MAX_REL), rb[:, ::-1],
                         fill(0, tq - MAX_REL - 1), fill(2 * MAX_REL, tq)], axis=1)
    assert g.shape[1] == ATT_BIAS_PERIOD
    return g


def _attention(att_qkv, bias_gen, batch, seq):
    m = att_qkv.shape[0]
    tq = ATT_Q_BLOCK
    n_sub = ATT_Q_PER_STEP
    assert LEFT_CHUNKS * CHUNK == 2 * tq and seq % (n_sub * tq) == 0
    nq = seq // tq
    n_steps = nq // n_sub
    qspec = pl.BlockSpec((n_sub * tq, ATT_WIDTH), lambda b, i: (b * n_steps + i, 0))

    def kv(col, off):
        return pl.BlockSpec((tq, ATT_WIDTH),
                            lambda b, i: (b * nq + jnp.maximum(i * n_sub - 2 + off, 0), col))

    n_kv = n_sub + 2
    return pl.pallas_call(
        _att_kernel,
        grid=(batch, n_steps),
        in_specs=([qspec] + [kv(1, j) for j in range(n_kv)] + [kv(2, j) for j in range(n_kv)]
                  + [_resident((ATT_HEADS, ATT_BIAS_PERIOD))]),
        out_specs=pl.BlockSpec((n_sub * tq, ATT_WIDTH), lambda b, i: (b * n_steps + i, 0)),
        out_shape=jax.ShapeDtypeStruct((m, ATT_WIDTH), BF16),
        scratch_shapes=[pltpu.VMEM((ATT_HEADS, tq, 3 * tq), F32)],
        compiler_params=pltpu.CompilerParams(
            dimension_semantics=("arbitrary", "arbitrary"),
            vmem_limit_bytes=_vmem_limit(40 * 1024 * 1024)),
        name="band_attention",
    )(*([att_qkv] * (1 + 2 * n_kv)), bias_gen)


def _ret_kernel(q_ref, k_ref, v_ref, g_ref, dm_ref, qdec_ref, kdec_ref, gblk_ref, o_ref, state_ref):
    @pl.when(pl.program_id(1) == 0)
    def _():
        state_ref[...] = jnp.zeros_like(state_ref)

    lane = lax.broadcasted_iota(jnp.int32, (1, LANES), 1)
    sub = lax.broadcasted_iota(jnp.int32, (LANES, 1), 0)
    for sb in range(RET_BATCH_PER_STEP):
        q = q_ref[sb]
        k = k_ref[sb]
        kd = k.astype(F32) * kdec_ref[...]
        for p in range(RET_QK_WIDTH // LANES):
            qp = q[:, p * LANES:(p + 1) * LANES]
            kp = k[:, p * LANES:(p + 1) * LANES]
            kd_t = kd[:, p * LANES:(p + 1) * LANES].T
            st = state_ref[sb, p]
            st_b = st.astype(BF16)
            new_st = st * gblk_ref[p]
            for hh in range(2):
                h = 2 * p + hh
                qm = qp * ((lane // RET_QK_DIM) == hh).astype(BF16)
                s = lax.dot_general(qm, kp, (((1,), (1,)), ((), ())), preferred_element_type=F32)
                sd = (s * dm_ref[h]).astype(BF16)
                vh = v_ref[sb, :, h * RET_V_DIM:(h + 1) * RET_V_DIM]
                inner = jnp.dot(sd, vh, preferred_element_type=F32)
                cross = jnp.dot(qm, st_b, preferred_element_type=F32) * qdec_ref[h]
                out = inner + cross
                mu = jnp.mean(out, -1, keepdims=True)
                oc = out - mu
                var = jnp.mean(oc * oc, -1, keepdims=True)
                nrm = oc * lax.rsqrt(var + LN_EPS)
                gate = g_ref[sb, :, h * RET_V_DIM:(h + 1) * RET_V_DIM].astype(F32)
                o_ref[sb, :, h * RET_V_DIM:(h + 1) * RET_V_DIM] = (nrm * gate).astype(BF16)
                kd_h = jnp.where((sub // RET_QK_DIM) == hh, kd_t, 0.0).astype(BF16)
                new_st = new_st + jnp.dot(kd_h, vh, preferred_element_type=F32)
            state_ref[sb, p] = new_st


def _ret_tables():
    t = RET_BLOCK
    log_g = jnp.log(1.0 - jnp.power(2.0, -5.0 - jnp.arange(RET_HEADS, dtype=F32)))
    pos = jnp.arange(t, dtype=F32)
    n, mm = pos[:, None], pos[None, :]
    cn, cm = jnp.floor(n / CHUNK), jnp.floor(mm / CHUNK)
    expo = jnp.where(cn == cm, jnp.abs(n - mm), n - mm)
    dm = jnp.where((cm <= cn)[None], jnp.exp(log_g[:, None, None] * jnp.where(cm <= cn, expo, 0.0)[None]), 0.0)
    qdec = jnp.exp(log_g[:, None] * (pos + 1.0)[None, :])
    qdec = jnp.broadcast_to(qdec[:, :, None], (RET_HEADS, t, RET_V_DIM))
    kdec = jnp.exp(log_g[:, None] * (t - 1.0 - pos)[None, :])
    kdec = jnp.repeat(kdec.T, RET_QK_DIM, axis=1)
    gblk = jnp.repeat(jnp.exp(log_g * t), RET_QK_DIM).reshape(RET_QK_WIDTH // LANES, LANES, 1)
    gblk = jnp.broadcast_to(gblk, (RET_QK_WIDTH // LANES, LANES, RET_V_DIM))
    return dm, qdec, kdec, gblk


def _retention(ret_qk, ret_v, ret_g, tables, batch, seq):
    m = ret_qk.shape[0]
    t = RET_BLOCK
    nsb = RET_BATCH_PER_STEP
    assert seq % t == 0 and t % CHUNK == 0 and batch % nsb == 0
    dm, qdec, kdec, gblk = tables
    n_pairs = RET_QK_WIDTH // LANES
    rows = lambda w, col: pl.BlockSpec((nsb, t, w), lambda b, i: (b, i, col))
    as3d = lambda a: a.reshape(batch, seq, a.shape[-1])
    out = pl.pallas_call(
        _ret_kernel,
        grid=(batch // nsb, seq // t),
        in_specs=[rows(RET_QK_WIDTH, 0), rows(RET_QK_WIDTH, 1), rows(RET_V_WIDTH, 0), rows(RET_V_WIDTH, 0),
                  _resident((RET_HEADS, t, t)), _resident((RET_HEADS, t, RET_V_DIM)),
                  _resident((t, RET_QK_WIDTH)), _resident((n_pairs, LANES, RET_V_DIM))],
        out_specs=rows(RET_V_WIDTH, 0),
        out_shape=jax.ShapeDtypeStruct((batch, seq, RET_V_WIDTH), BF16),
        scratch_shapes=[pltpu.VMEM((nsb, n_pairs, LANES, RET_V_DIM), F32)],
        compiler_params=pltpu.CompilerParams(
            dimension_semantics=("arbitrary", "arbitrary"),
            vmem_limit_bytes=_vmem_limit(40 * 1024 * 1024)),
        name="chunk_retention",
    )(as3d(ret_qk), as3d(ret_qk), as3d(ret_v), as3d(ret_g), dm, qdec, kdec, gblk)
    return out.reshape(m, RET_V_WIDTH)


def _s5_disc_kernel(lre_ref, lim_ref, lstep_ref, bre_ref, bim_ref, lbr_ref, lbi_ref, bbr_ref, bbi_ref):
    lre, lim = lre_ref[...], lim_ref[...]
    step = jnp.exp(lstep_ref[...])
    mag = jnp.exp(lre * step)
    ang = lim * step
    lbr = mag * jnp.cos(ang)
    lbi = mag * jnp.sin(ang)
    den = lre * lre + lim * lim
    nr = lbr - 1.0
    cr = (nr * lre + lbi * lim) / den
    ci = (lbi * lre - nr * lim) / den
    lbr_ref[...] = lbr
    lbi_ref[...] = lbi
    bre, bim = bre_ref[...], bim_ref[...]
    bbr_ref[...] = cr * bre - ci * bim
    bbi_ref[...] = cr * bim + ci * bre


def _s5_discretise(lam_re, lam_im, log_step, b_re, b_im):
    gp = SSM_GROUPS * SSM_STATE
    row = lambda a: a.reshape(1, gp).astype(F32)
    lstep = jnp.repeat(log_step.astype(F32), SSM_STATE).reshape(1, gp)
    b_t = lambda b: b.astype(F32).reshape(gp, SSM_GROUP).T
    full = lambda s: pl.BlockSpec(s, lambda: (0,) * len(s))
    return pl.pallas_call(
        _s5_disc_kernel,
        in_specs=[full((1, gp))] * 3 + [full((SSM_GROUP, gp))] * 2,
        out_specs=[full((1, gp))] * 2 + [full((SSM_GROUP, gp))] * 2,
        out_shape=[jax.ShapeDtypeStruct((1, gp), F32)] * 2 + [jax.ShapeDtypeStruct((SSM_GROUP, gp), F32)] * 2,
        name="s5_discretise",
    )(row(lam_re), row(lam_im), lstep, b_t(b_re), b_t(b_im))


def _s5_matrices(lbr, lbi, bbr, bbi, c_re, c_im, batch):
    nt, gl = SSM_LANE_TILES, LANES // SSM_GROUP
    eye = jnp.eye(gl, dtype=F32)

    def expand(bb):
        bb = bb.reshape(SSM_GROUP, nt, gl, SSM_STATE).transpose(1, 2, 0, 3)
        return jnp.einsum('tgip,gh->tgihp', bb, eye).reshape(nt, LANES, SSM_TILE_STATE)

    def contract(c):
        c = c.astype(F32).reshape(nt, gl, SSM_GROUP, SSM_STATE)
        return jnp.einsum('tgop,gh->tgpho', c, eye).reshape(nt, SSM_TILE_STATE, LANES)

    b_big = jnp.concatenate([expand(bbr), expand(bbi)], axis=-1).astype(BF16)
    c_big = jnp.concatenate([contract(c_re), -contract(c_im)], axis=1).astype(BF16)
    lam_r = jnp.broadcast_to(lbr, (batch, SSM_GROUPS * SSM_STATE))
    lam_i = jnp.broadcast_to(lbi, (batch, SSM_GROUPS * SSM_STATE))
    return b_big, c_big, lam_r, lam_i


def _s5_kernel(batch, u_ref, bbig_ref, cbig_ref, lr_ref, li_ref, dskip_ref, wglu_ref, y_ref,
               x_ref, st_ref, tb_ref):
    @pl.when(pl.program_id(0) == 0)
    def _():
        st_ref[...] = jnp.zeros_like(st_ref)

    ts = SSM_TILE_STATE
    tt = SSM_T_BLOCK
    blocks = range(SSM_BLOCKS_PER_STEP)
    tiles = range(SSM_LANE_TILES)
    for blk in blocks:
        for b in range(batch):
            for kt in tiles:
                c0 = b * SSM_WIDTH + kt * LANES
                tb_ref[blk, kt, pl.ds(b, tt, stride=batch), :] = (
                    u_ref[blk * tt:(blk + 1) * tt, c0:c0 + LANES].astype(F32))
    u_f32 = [[tb_ref[blk, kt] for kt in tiles] for blk in blocks]
    for blk in blocks:
        for kt in tiles:
            x_ref[blk, :, 2 * ts * kt:2 * ts * (kt + 1)] = jnp.dot(
                u_f32[blk][kt].astype(BF16), bbig_ref[kt], preferred_element_type=F32)

    for blk in blocks:
        for kt in tiles:
            re0, im0 = 2 * ts * kt, 2 * ts * kt + ts
            lr = lr_ref[:, kt * ts:(kt + 1) * ts]
            li = li_ref[:, kt * ts:(kt + 1) * ts]
            xr, xi = st_ref[:, re0:re0 + ts], st_ref[:, im0:im0 + ts]
            for t in range(tt):
                rows = slice(t * batch, (t + 1) * batch)
                xr, xi = (lr * xr - li * xi + x_ref[blk, rows, re0:re0 + ts],
                          lr * xi + li * xr + x_ref[blk, rows, im0:im0 + ts])
                x_ref[blk, rows, re0:re0 + ts] = xr
                x_ref[blk, rows, im0:im0 + ts] = xi
            st_ref[:, re0:re0 + ts] = xr
            st_ref[:, im0:im0 + ts] = xi

    for blk in blocks:
        ys = []
        for kt in tiles:
            xs = x_ref[blk, :, 2 * ts * kt:2 * ts * (kt + 1)].astype(BF16)
            ys.append(jnp.dot(xs, cbig_ref[kt], preferred_element_type=F32))
        y = jnp.concatenate(ys, axis=-1) + dskip_ref[...] * jnp.concatenate(u_f32[blk], axis=-1)
        z = _gelu_x2(y)
        y = (0.5 * z) * jax.nn.sigmoid(jnp.dot(z.astype(BF16), wglu_ref[...], preferred_element_type=F32))
        for kt in tiles:
            tb_ref[blk, kt] = y[:, kt * LANES:(kt + 1) * LANES]
        for b in range(batch):
            for kt in tiles:
                c0 = b * SSM_WIDTH + kt * LANES
                y_ref[blk * tt:(blk + 1) * tt, c0:c0 + LANES] = (
                    tb_ref[blk, kt, pl.ds(b, tt, stride=batch), :].astype(BF16))


def _s5(u_sb, mats, d_skip, w_glu, batch, seq):
    b_big, c_big, lam_r, lam_i = mats
    nblk = SSM_BLOCKS_PER_STEP
    rows = SSM_T_BLOCK * batch
    assert seq % (nblk * SSM_T_BLOCK) == 0 and batch % 8 == 0
    n_state = 2 * SSM_TILE_STATE * SSM_LANE_TILES
    blk = pl.BlockSpec((nblk * SSM_T_BLOCK, batch * SSM_WIDTH), lambda j: (j, 0))
    return pl.pallas_call(
        functools.partial(_s5_kernel, batch),
        grid=(seq // (nblk * SSM_T_BLOCK),),
        in_specs=[blk, _resident(b_big.shape), _resident(c_big.shape), _resident(lam_r.shape),
                  _resident(lam_i.shape), _resident((1, SSM_WIDTH)), _resident((SSM_WIDTH, SSM_WIDTH))],
        out_specs=blk,
        out_shape=jax.ShapeDtypeStruct((seq, batch * SSM_WIDTH), BF16),
        scratch_shapes=[pltpu.VMEM((nblk, rows, n_state), F32), pltpu.VMEM((batch, n_state), F32),
                        pltpu.VMEM((nblk, SSM_LANE_TILES, rows, LANES), F32)],
        compiler_params=pltpu.CompilerParams(
            dimension_semantics=("arbitrary",), vmem_limit_bytes=_vmem_limit(56 * 1024 * 1024)),
        name="s5_scan",
    )(u_sb, b_big, c_big, lam_r, lam_i, d_skip, w_glu)


MERGE_ROW_TILE = 2 * ROW_TILE


def _merge_kernel(att_ref, ret_ref, yc_ref, h_ref, wg_ref, wa_ref, wb_ref, wc_ref, wo_ref,
                  g_ref, b_ref, out_ref):
    d = D_MODEL
    for r0 in range(0, MERGE_ROW_TILE, ROW_TILE):
        rows = slice(r0, r0 + ROW_TILE)
        h = h_ref[rows, :]
        hb = h.astype(BF16)
        merged = None
        for i, (x_ref, w_ref) in enumerate(((att_ref, wa_ref), (ret_ref, wb_ref), (yc_ref, wc_ref))):
            gate = jax.nn.sigmoid(jnp.dot(hb, wg_ref[:, i * d:(i + 1) * d], preferred_element_type=F32))
            term = gate * jnp.dot(x_ref[rows, :], w_ref[...], preferred_element_type=F32)
            merged = term if merged is None else merged + term
        y = DEEPNORM_ALPHA * h + jnp.dot(merged.astype(BF16), wo_ref[...], preferred_element_type=F32)
        out_ref[rows, :] = _layer_norm(y, g_ref[...], b_ref[...])


def _merge(att_o, ret_o, yc_sb, h, w_gate, wa, wb, wc, wo, ln_g, ln_b, batch, seq):
    m = h.shape[0]
    tm = MERGE_ROW_TILE
    assert seq % tm == 0
    n_s = seq // tm
    row = lambda w: pl.BlockSpec((tm, w), lambda i: (i, 0))
    return pl.pallas_call(
        _merge_kernel,
        grid=(m // tm,),
        in_specs=[row(ATT_WIDTH), row(RET_V_WIDTH),
                  pl.BlockSpec((tm, SSM_WIDTH), lambda i: (i % n_s, i // n_s)),
                  row(D_MODEL), _resident(w_gate.shape),
                  _resident(wa.shape), _resident(wb.shape), _resident(wc.shape), _resident(wo.shape),
                  _resident((1, D_MODEL)), _resident((1, D_MODEL))],
        out_specs=row(D_MODEL),
        out_shape=jax.ShapeDtypeStruct((m, D_MODEL), F32),
        compiler_params=pltpu.CompilerParams(
            dimension_semantics=("arbitrary",), vmem_limit_bytes=_vmem_limit(54 * 1024 * 1024)),
        name="gated_merge",
    )(att_o, ret_o, yc_sb, h, w_gate, wa, wb, wc, wo, ln_g, ln_b)


FF_T_BLOCK = 128
FF_ROW_PARTS = 4


def _ffn_kernel(batch, h_ref, wup_ref, cw_ref, cb_ref, wdn_ref, g_ref, b_ref, out_ref,
                tb_ref, halo_ref, chunk_ref, act_ref):
    tt = FF_T_BLOCK
    rows = tt * batch
    halo = 2 * batch
    n_lane_tiles = D_MODEL // LANES

    @pl.when(pl.program_id(0) == 0)
    def _():
        halo_ref[...] = jnp.zeros_like(halo_ref)

    for b in range(batch):
        for c in range(n_lane_tiles):
            tb_ref[c, pl.ds(b, tt, stride=batch), :] = h_ref[b, :, c * LANES:(c + 1) * LANES]
    hb = jnp.concatenate([tb_ref[c] for c in range(n_lane_tiles)], axis=-1).astype(BF16)

    def conv(buf, col):
        cols = slice(col, col + FF_CHUNK)
        buf[0:halo, :] = halo_ref[:, cols]
        buf[halo:halo + rows, :] = jnp.dot(hb, wup_ref[:, cols], preferred_element_type=F32)
        halo_ref[:, cols] = buf[rows:rows + halo, :]
        w = cw_ref[:, cols]
        return (buf[0:rows, :] * w[0:1] + buf[batch:batch + rows, :] * w[1:2]
                + buf[halo:halo + rows, :] * w[2:3] + cb_ref[:, cols])

    for j in range(D_FF // FF_CHUNK):
        a = conv(chunk_ref.at[j % 2, 0], j * FF_CHUNK)
        g_half = conv(chunk_ref.at[j % 2, 1], D_FF + j * FF_CHUNK)
        act_ref[:, j * FF_CHUNK:(j + 1) * FF_CHUNK] = (_gelu_x2(a) * g_half).astype(BF16)

    tp = tt // FF_ROW_PARTS
    nr = tp * batch
    for part in range(FF_ROW_PARTS):
        r0 = part * nr
        h = jnp.concatenate([tb_ref[c, r0:r0 + nr, :] for c in range(n_lane_tiles)], axis=-1)
        y = DEEPNORM_ALPHA * h + jnp.dot(act_ref[r0:r0 + nr, :], wdn_ref[...], preferred_element_type=F32)
        y = _layer_norm(y, g_ref[...], b_ref[...])
        for c in range(n_lane_tiles):
            tb_ref[c, r0:r0 + nr, :] = y[:, c * LANES:(c + 1) * LANES]
        for b in range(batch):
            for c in range(n_lane_tiles):
                out_ref[b, part * tp:(part + 1) * tp, c * LANES:(c + 1) * LANES] = (
                    tb_ref[c, pl.ds(r0 + b, tp, stride=batch), :])


def _ffn(h, w_up, conv_w, conv_b, w_down, ln_g, ln_b, batch, seq):
    tt = FF_T_BLOCK
    rows = tt * batch
    assert D_FF % FF_CHUNK == 0 and tt % FF_ROW_PARTS == 0 and seq % tt == 0 and batch % 8 == 0
    half_gate = jnp.concatenate([jnp.ones((D_FF,), F32), jnp.full((D_FF,), 0.5, F32)])
    blk = pl.BlockSpec((batch, tt, D_MODEL), lambda j: (0, j, 0))
    out = pl.pallas_call(
        functools.partial(_ffn_kernel, batch),
        grid=(seq // tt,),
        in_specs=[blk, _resident(w_up.shape), _resident(conv_w.shape), _resident((1, 2 * D_FF)),
                  _resident(w_down.shape), _resident((1, D_MODEL)), _resident((1, D_MODEL))],
        out_specs=blk,
        out_shape=jax.ShapeDtypeStruct((batch, seq, D_MODEL), F32),
        scratch_shapes=[pltpu.VMEM((D_MODEL // LANES, rows, LANES), F32),
                        pltpu.VMEM((2 * batch, 2 * D_FF), F32),
                        pltpu.VMEM((2, 2, rows + 2 * batch, FF_CHUNK), F32),
                        pltpu.VMEM((rows, D_FF), BF16)],
        compiler_params=pltpu.CompilerParams(
            dimension_semantics=("arbitrary",), vmem_limit_bytes=_vmem_limit(58 * 1024 * 1024)),
        name="conv_ffn",
    )(h.reshape(batch, seq, D_MODEL), w_up, conv_w * half_gate, conv_b * half_gate, w_down, ln_g, ln_b)
    return out.reshape(batch * seq, D_MODEL)


def _rotary_tables(seq):
    half = RET_QK_DIM // 2
    inv = ROPE_BASE ** (-jnp.arange(0, RET_QK_DIM, 2, dtype=F32) / RET_QK_DIM)
    ang = jnp.arange(seq, dtype=F32)[:, None] * inv[None, :]
    cos, sin = jnp.cos(ang), jnp.sin(ang)
    assert cos.shape[1] == half
    heads_per_tile = LANES // RET_QK_DIM
    cos_t = jnp.tile(jnp.concatenate([cos, cos], -1), (1, heads_per_tile))
    sin_t = jnp.tile(jnp.concatenate([-sin, sin], -1), (1, heads_per_tile))
    return cos_t, sin_t


def kernel(x, ln_in_g, ln_in_b, w_in, rel_bias, w_proj_a, w_proj_b, w_proj_c, lam_re, lam_im, log_step,
           b_re, b_im, c_re, c_im, d_skip, w_glu, w_o, ln1_g, ln1_b, w_up, conv_w, conv_b, w_down,
           ln2_g, ln2_b):
    batch, seq, d = x.shape
    assert d == D_MODEL and w_in.shape == (DEPTH, D_MODEL, IN_WIDTH)
    m = batch * seq
    vec = lambda a: a.reshape(1, -1).astype(F32)
    cos_t, sin_t = _rotary_tables(seq)
    ret_tabs = _ret_tables()
    h = x.reshape(m, d)
    for l in range(DEPTH):
        w_mix = w_in[l, :, :MIX_WIDTH].astype(BF16)
        w_gate = w_in[l, :, MIX_WIDTH:].astype(BF16)
        if l == 0:
            h, att_qkv, ret_qk, ret_v, ret_g, u_sb = _in_proj(
                h, vec(ln_in_g), vec(ln_in_b), w_mix, cos_t, sin_t, batch, seq)
        else:
            att_qkv, ret_qk, ret_v, ret_g, u_sb = _in_proj(
                h, None, None, w_mix, cos_t, sin_t, batch, seq)
        att_o = _attention(att_qkv, _att_bias_generator(rel_bias[l]), batch, seq)
        ret_o = _retention(ret_qk, ret_v, ret_g, ret_tabs, batch, seq)
        lbr, lbi, bbr, bbi = _s5_discretise(lam_re[l], lam_im[l], log_step[l], b_re[l], b_im[l])
        mats = _s5_matrices(lbr, lbi, bbr, bbi, c_re[l], c_im[l], batch)
        yc_sb = _s5(u_sb, mats, vec(d_skip[l]), (0.5 * w_glu[l]).astype(BF16), batch, seq)
        h = _merge(att_o, ret_o, yc_sb, h, w_gate, w_proj_a[l].astype(BF16), w_proj_b[l].astype(BF16),
                   w_proj_c[l].astype(BF16), w_o[l].astype(BF16), vec(ln1_g[l]), vec(ln1_b[l]), batch, seq)
        h = _ffn(h, w_up[l].astype(BF16), conv_w[l].astype(F32), vec(conv_b[l]), w_down[l].astype(BF16),
                 vec(ln2_g[l]), vec(ln2_b[l]), batch, seq)
    return h.reshape(batch, seq, d)
```

```python
import functools
import math

import jax
import jax.numpy as jnp
from jax import lax
from jax.experimental import pallas as pl
from jax.experimental.pallas import tpu as pltpu

F32 = jnp.float32
BF16 = jnp.bfloat16

D_MODEL = 1024
DEPTH = 2
CHUNK = 64
ATT_HEADS = 8
ATT_HEAD_DIM = 64
ATT_WIDTH = ATT_HEADS * ATT_HEAD_DIM
LEFT_CHUNKS = 8
MAX_REL = 128
RET_HEADS = 8
RET_QK_DIM = 64
RET_V_DIM = 128
RET_QK_WIDTH = RET_HEADS * RET_QK_DIM
RET_V_WIDTH = RET_HEADS * RET_V_DIM
ROPE_BASE = 10000.0
SSM_WIDTH = 512
SSM_GROUP = 16
SSM_GROUPS = SSM_WIDTH // SSM_GROUP
SSM_STATE = 64
D_FF = 2816
N_BRANCH = 3
IN_WIDTH = 8192
MIX_WIDTH = IN_WIDTH - N_BRANCH * D_MODEL
DEEPNORM_ALPHA = (2.0 * DEPTH) ** 0.25
LN_EPS = 1e-5
MASK_VALUE = -1e30
LOG2E = math.log2(math.e)

LANES = 128
V7X_VMEM_BYTES = 64 * 1024 * 1024

ROW_TILE = 512
IN_ROW_TILE = 1024
COL_CHUNK = 512
ATT_Q_BLOCK = 256
RET_BLOCK = 256
RET_BATCH_PER_STEP = 4
SSM_T_BLOCK = 64
SSM_BLOCKS_PER_STEP = 4
SSM_LANE_TILES = SSM_WIDTH // LANES
SSM_TILE_STATE = (LANES // SSM_GROUP) * SSM_STATE
FF_CHUNK = 256


def _vmem_limit(nbytes):
    return int(min(nbytes, V7X_VMEM_BYTES - 6 * 1024 * 1024))


def _resident(shape):
    nd = len(shape)
    return pl.BlockSpec(shape, lambda *_: (0,) * nd, pipeline_mode=pl.Buffered(1))


def _layer_norm(x, g, b):
    mu = jnp.mean(x, -1, keepdims=True)
    xc = x - mu
    var = jnp.mean(xc * xc, -1, keepdims=True)
    return xc * lax.rsqrt(var + LN_EPS) * g + b


GELU_C1 = math.sqrt(2.0 / math.pi)
GELU_C2 = 0.044715 * GELU_C1


def _gelu_x2(x):
    return x + x * jnp.tanh(x * (GELU_C1 + GELU_C2 * (x * x)))


def _in_proj_kernel(apply_ln, *refs):
    if apply_ln:
        (x_ref, g_ref, b_ref, w_ref, cos_ref, sin_ref,
         hf_ref, att_ref, rqk_ref, rv_ref, rg_ref, u_ref) = refs
    else:
        (x_ref, w_ref, cos_ref, sin_ref,
         att_ref, rqk_ref, rv_ref, rg_ref, u_ref) = refs
    x = x_ref[...]
    if apply_ln:
        x = _layer_norm(x, g_ref[...], b_ref[...])
        hf_ref[...] = x
    xb = x.astype(BF16)

    def proj(c):
        return jnp.dot(xb, w_ref[:, c * COL_CHUNK:(c + 1) * COL_CHUNK], preferred_element_type=F32)

    lane = lax.broadcasted_iota(jnp.int32, (1, COL_CHUNK), 1)
    first_half = (lane % RET_QK_DIM) < (RET_QK_DIM // 2)
    half = RET_QK_DIM // 2

    cos_t = jnp.concatenate([cos_ref[...]] * (COL_CHUNK // LANES), axis=1)
    sin_t = jnp.concatenate([sin_ref[...]] * (COL_CHUNK // LANES), axis=1)

    def rotary(a):
        swapped = jnp.where(first_half, pltpu.roll(a, COL_CHUNK - half, 1), pltpu.roll(a, half, 1))
        return a * cos_t + swapped * sin_t

    att_ref[:, 0:512] = (proj(0) * (ATT_HEAD_DIM ** -0.5 * LOG2E)).astype(BF16)
    att_ref[:, 512:1024] = proj(1).astype(BF16)
    att_ref[:, 1024:1536] = proj(2).astype(BF16)
    rqk_ref[:, 0:512] = (rotary(proj(3)) * (RET_QK_DIM ** -0.5)).astype(BF16)
    rqk_ref[:, 512:1024] = rotary(proj(4)).astype(BF16)
    rv_ref[:, 0:512] = proj(5).astype(BF16)
    rv_ref[:, 512:1024] = proj(6).astype(BF16)
    for c in range(2):
        a = proj(7 + c)
        rg_ref[:, c * 512:(c + 1) * 512] = (a * jax.nn.sigmoid(a)).astype(BF16)
    u_ref[...] = proj(9).astype(BF16)


def _in_proj(x2d, ln_g, ln_b, w_bf16, cos_t, sin_t, batch, seq):
    m = x2d.shape[0]
    tm = IN_ROW_TILE
    assert seq % tm == 0 and w_bf16.shape == (D_MODEL, MIX_WIDTH)
    n_s = seq // tm
    apply_ln = ln_g is not None
    row = lambda w: pl.BlockSpec((tm, w), lambda i: (i, 0))
    in_specs = [row(D_MODEL)]
    args = [x2d]
    if apply_ln:
        in_specs += [_resident((1, D_MODEL)), _resident((1, D_MODEL))]
        args += [ln_g, ln_b]
    in_specs += [_resident((D_MODEL, MIX_WIDTH)),
                 pl.BlockSpec((tm, LANES), lambda i: (i % n_s, 0)),
                 pl.BlockSpec((tm, LANES), lambda i: (i % n_s, 0))]
    args += [w_bf16, cos_t, sin_t]
    out_shape, out_specs = [], []
    if apply_ln:
        out_shape.append(jax.ShapeDtypeStruct((m, D_MODEL), F32))
        out_specs.append(row(D_MODEL))
    for w in (3 * ATT_WIDTH, 2 * RET_QK_WIDTH, RET_V_WIDTH, RET_V_WIDTH):
        out_shape.append(jax.ShapeDtypeStruct((m, w), BF16))
        out_specs.append(row(w))
    out_shape.append(jax.ShapeDtypeStruct((seq, batch * SSM_WIDTH), BF16))
    out_specs.append(pl.BlockSpec((tm, SSM_WIDTH), lambda i: (i % n_s, i // n_s)))
    return pl.pallas_call(
        functools.partial(_in_proj_kernel, apply_ln),
        grid=(m // tm,),
        in_specs=in_specs,
        out_specs=out_specs,
        out_shape=out_shape,
        compiler_params=pltpu.CompilerParams(
            dimension_semantics=("arbitrary",), vmem_limit_bytes=_vmem_limit(56 * 1024 * 1024)),
        name="in_proj_ln" if apply_ln else "in_proj",
    )(*args)


ATT_Q_PER_STEP = 8


def _att_kernel(q_ref, *refs):
    n_kv = ATT_Q_PER_STEP + 2
    k_refs, v_refs, (gen_ref, o_ref, bias_ref) = refs[:n_kv], refs[n_kv:2 * n_kv], refs[2 * n_kv:]
    tq = ATT_Q_BLOCK

    @pl.when((pl.program_id(0) == 0) & (pl.program_id(1) == 0))
    def _():
        qc = lax.broadcasted_iota(jnp.int32, (tq, 3 * tq), 0) // CHUNK
        kc = lax.broadcasted_iota(jnp.int32, (tq, 3 * tq), 1) // CHUNK
        in_band = (kc >= qc) & (kc <= qc + LEFT_CHUNKS)
        for h in range(ATT_HEADS):
            gen = jnp.broadcast_to(gen_ref[h:h + 1, :], (tq, ATT_BIAS_PERIOD))
            toep = pltpu.roll(gen, 0, 1, stride=1, stride_axis=0)
            bias_ref[h] = jnp.where(in_band, toep[:, :3 * tq], MASK_VALUE)

    k_all = jnp.concatenate([r[...] for r in k_refs], axis=0)
    v_all = jnp.concatenate([r[...] for r in v_refs], axis=0)
    lane = lax.broadcasted_iota(jnp.int32, (1, LANES), 1)
    first = (lane == 0).astype(BF16)
    row = lax.broadcasted_iota(jnp.int32, (3 * tq, 1), 0)
    q_one = jnp.broadcast_to(first, (tq, LANES))
    v_ones = jnp.ones((3 * tq, LANES), BF16)
    for sub in range(ATT_Q_PER_STEP):
        qb = pl.program_id(1) * ATT_Q_PER_STEP + sub
        q = q_ref[sub * tq:(sub + 1) * tq, :]
        k = k_all[sub * tq:(sub + 3) * tq, :]
        v = v_all[sub * tq:(sub + 3) * tq, :]
        pen = jnp.where(row < tq, jnp.where(qb >= 2, 0.0, MASK_VALUE),
                        jnp.where(row < 2 * tq, jnp.where(qb >= 1, 0.0, MASK_VALUE), 0.0))
        k_pen = pen.astype(BF16) * first
        for p in range(ATT_WIDTH // LANES):
            qp = q[:, p * LANES:(p + 1) * LANES]
            k_ext = jnp.concatenate([k[:, p * LANES:(p + 1) * LANES], k_pen], axis=1)
            v_ext = jnp.concatenate([v[:, p * LANES:(p + 1) * LANES], v_ones], axis=1)
            outs = []
            for hh in range(2):
                h = 2 * p + hh
                head_lanes = ((lane // ATT_HEAD_DIM) == hh).astype(BF16)
                q_ext = jnp.concatenate([qp * head_lanes, q_one], axis=1)
                s = lax.dot_general(q_ext, k_ext, (((1,), (1,)), ((), ())), preferred_element_type=F32)
                s = s + bias_ref[h]
                e = jnp.exp2(s - jnp.max(s, -1, keepdims=True))
                o = jnp.dot(e.astype(BF16), v_ext, preferred_element_type=F32)
                outs.append(o[:, :LANES] / o[:, LANES:])
            o_ref[sub * tq:(sub + 1) * tq, p * LANES:(p + 1) * LANES] = jnp.where(
                (lane // ATT_HEAD_DIM) == 0, outs[0], outs[1]).astype(BF16)


ATT_BIAS_PERIOD = 4 * ATT_Q_BLOCK


def _att_bias_generator(rel_bias):
    tq = ATT_Q_BLOCK
    rb = rel_bias.astype(F32) * LOG2E
    nh = rb.shape[0]
    fill = lambda col, n: jnp.broadcast_to(rb[:, col:col + 1], (nh, n))
    g = jnp.concatenate([fill(2 * MAX_REL, 2 * tq - MAX_REL), rb[:, ::-1],
                         fill(0, tq - MAX_REL - 1), fill(2 * MAX_REL, tq)], axis=1)
    assert g.shape[1] == ATT_BIAS_PERIOD
    return g


def _attention(att_qkv, bias_gen, batch, seq):
    m = att_qkv.shape[0]
    tq = ATT_Q_BLOCK
    n_sub = ATT_Q_PER_STEP
    assert LEFT_CHUNKS * CHUNK == 2 * tq and seq % (n_sub * tq) == 0
    nq = seq // tq
    n_steps = nq // n_sub
    qspec = pl.BlockSpec((n_sub * tq, ATT_WIDTH), lambda b, i: (b * n_steps + i, 0))

    def kv(col, off):
        return pl.BlockSpec((tq, ATT_WIDTH),
                            lambda b, i: (b * nq + jnp.maximum(i * n_sub - 2 + off, 0), col))

    n_kv = n_sub + 2
    return pl.pallas_call(
        _att_kernel,
        grid=(batch, n_steps),
        in_specs=([qspec] + [kv(1, j) for j in range(n_kv)] + [kv(2, j) for j in range(n_kv)]
                  + [_resident((ATT_HEADS, ATT_BIAS_PERIOD))]),
        out_specs=pl.BlockSpec((n_sub * tq, ATT_WIDTH), lambda b, i: (b * n_steps + i, 0)),
        out_shape=jax.ShapeDtypeStruct((m, ATT_WIDTH), BF16),
        scratch_shapes=[pltpu.VMEM((ATT_HEADS, tq, 3 * tq), F32)],
        compiler_params=pltpu.CompilerParams(
            dimension_semantics=("arbitrary", "arbitrary"),
            vmem_limit_bytes=_vmem_limit(40 * 1024 * 1024)),
        name="band_attention",
    )(*([att_qkv] * (1 + 2 * n_kv)), bias_gen)


def _ret_kernel(q_ref, k_ref, v_ref, g_ref, dm_ref, qdec_ref, kdec_ref, gblk_ref, o_ref, state_ref):
    @pl.when(pl.program_id(1) == 0)
    def _():
        state_ref[...] = jnp.zeros_like(state_ref)

    lane = lax.broadcasted_iota(jnp.int32, (1, LANES), 1)
    sub = lax.broadcasted_iota(jnp.int32, (LANES, 1), 0)
    for sb in range(RET_BATCH_PER_STEP):
        q = q_ref[sb]
        k = k_ref[sb]
        kd = k.astype(F32) * kdec_ref[...]
        for p in range(RET_QK_WIDTH // LANES):
            qp = q[:, p * LANES:(p + 1) * LANES]
            kp = k[:, p * LANES:(p + 1) * LANES]
            kd_t = kd[:, p * LANES:(p + 1) * LANES].T
            st = state_ref[sb, p]
            st_b = st.astype(BF16)
            new_st = st * gblk_ref[p]
            for hh in range(2):
                h = 2 * p + hh
                qm = qp * ((lane // RET_QK_DIM) == hh).astype(BF16)
                s = lax.dot_general(qm, kp, (((1,), (1,)), ((), ())), preferred_element_type=F32)
                sd = (s * dm_ref[h]).astype(BF16)
                vh = v_ref[sb, :, h * RET_V_DIM:(h + 1) * RET_V_DIM]
                inner = jnp.dot(sd, vh, preferred_element_type=F32)
                cross = jnp.dot(qm, st_b, preferred_element_type=F32) * qdec_ref[h]
                out = inner + cross
                mu = jnp.mean(out, -1, keepdims=True)
                oc = out - mu
                var = jnp.mean(oc * oc, -1, keepdims=True)
                nrm = oc * lax.rsqrt(var + LN_EPS)
                gate = g_ref[sb, :, h * RET_V_DIM:(h + 1) * RET_V_DIM].astype(F32)
                o_ref[sb, :, h * RET_V_DIM:(h + 1) * RET_V_DIM] = (nrm * gate).astype(BF16)
                kd_h = jnp.where((sub // RET_QK_DIM) == hh, kd_t, 0.0).astype(BF16)
                new_st = new_st + jnp.dot(kd_h, vh, preferred_element_type=F32)
            state_ref[sb, p] = new_st


def _ret_tables():
    t = RET_BLOCK
    log_g = jnp.log(1.0 - jnp.power(2.0, -5.0 - jnp.arange(RET_HEADS, dtype=F32)))
    pos = jnp.arange(t, dtype=F32)
    n, mm = pos[:, None], pos[None, :]
    cn, cm = jnp.floor(n / CHUNK), jnp.floor(mm / CHUNK)
    expo = jnp.where(cn == cm, jnp.abs(n - mm), n - mm)
    dm = jnp.where((cm <= cn)[None], jnp.exp(log_g[:, None, None] * jnp.where(cm <= cn, expo, 0.0)[None]), 0.0)
    qdec = jnp.exp(log_g[:, None] * (pos + 1.0)[None, :])
    qdec = jnp.broadcast_to(qdec[:, :, None], (RET_HEADS, t, RET_V_DIM))
    kdec = jnp.exp(log_g[:, None] * (t - 1.0 - pos)[None, :])
    kdec = jnp.repeat(kdec.T, RET_QK_DIM, axis=1)
    gblk = jnp.repeat(jnp.exp(log_g * t), RET_QK_DIM).reshape(RET_QK_WIDTH // LANES, LANES, 1)
    gblk = jnp.broadcast_to(gblk, (RET_QK_WIDTH // LANES, LANES, RET_V_DIM))
    return dm, qdec, kdec, gblk


def _retention(ret_qk, ret_v, ret_g, tables, batch, seq):
    m = ret_qk.shape[0]
    t = RET_BLOCK
    nsb = RET_BATCH_PER_STEP
    assert seq % t == 0 and t % CHUNK == 0 and batch % nsb == 0
    dm, qdec, kdec, gblk = tables
    n_pairs = RET_QK_WIDTH // LANES
    rows = lambda w, col: pl.BlockSpec((nsb, t, w), lambda b, i: (b, i, col))
    as3d = lambda a: a.reshape(batch, seq, a.shape[-1])
    out = pl.pallas_call(
        _ret_kernel,
        grid=(batch // nsb, seq // t),
        in_specs=[rows(RET_QK_WIDTH, 0), rows(RET_QK_WIDTH, 1), rows(RET_V_WIDTH, 0), rows(RET_V_WIDTH, 0),
                  _resident((RET_HEADS, t, t)), _resident((RET_HEADS, t, RET_V_DIM)),
                  _resident((t, RET_QK_WIDTH)), _resident((n_pairs, LANES, RET_V_DIM))],
        out_specs=rows(RET_V_WIDTH, 0),
        out_shape=jax.ShapeDtypeStruct((batch, seq, RET_V_WIDTH), BF16),
        scratch_shapes=[pltpu.VMEM((nsb, n_pairs, LANES, RET_V_DIM), F32)],
        compiler_params=pltpu.CompilerParams(
            dimension_semantics=("arbitrary", "arbitrary"),
            vmem_limit_bytes=_vmem_limit(40 * 1024 * 1024)),
        name="chunk_retention",
    )(as3d(ret_qk), as3d(ret_qk), as3d(ret_v), as3d(ret_g), dm, qdec, kdec, gblk)
    return out.reshape(m, RET_V_WIDTH)


def _s5_disc_kernel(lre_ref, lim_ref, lstep_ref, bre_ref, bim_ref, lbr_ref, lbi_ref, bbr_ref, bbi_ref):
    lre, lim = lre_ref[...], lim_ref[...]
    step = jnp.exp(lstep_ref[...])
    mag = jnp.exp(lre * step)
    ang = lim * step
    lbr = mag * jnp.cos(ang)
    lbi = mag * jnp.sin(ang)
    den = lre * lre + lim * lim
    nr = lbr - 1.0
    cr = (nr * lre + lbi * lim) / den
    ci = (lbi * lre - nr * lim) / den
    lbr_ref[...] = lbr
    lbi_ref[...] = lbi
    bre, bim = bre_ref[...], bim_ref[...]
    bbr_ref[...] = cr * bre - ci * bim
    bbi_ref[...] = cr * bim + ci * bre


def _s5_discretise(lam_re, lam_im, log_step, b_re, b_im):
    gp = SSM_GROUPS * SSM_STATE
    row = lambda a: a.reshape(1, gp).astype(F32)
    lstep = jnp.repeat(log_step.astype(F32), SSM_STATE).reshape(1, gp)
    b_t = lambda b: b.astype(F32).reshape(gp, SSM_GROUP).T
    full = lambda s: pl.BlockSpec(s, lambda: (0,) * len(s))
    return pl.pallas_call(
        _s5_disc_kernel,
        in_specs=[full((1, gp))] * 3 + [full((SSM_GROUP, gp))] * 2,
        out_specs=[full((1, gp))] * 2 + [full((SSM_GROUP, gp))] * 2,
        out_shape=[jax.ShapeDtypeStruct((1, gp), F32)] * 2 + [jax.ShapeDtypeStruct((SSM_GROUP, gp), F32)] * 2,
        name="s5_discretise",
    )(row(lam_re), row(lam_im), lstep, b_t(b_re), b_t(b_im))


def _s5_matrices(lbr, lbi, bbr, bbi, c_re, c_im, batch):
    nt, gl = SSM_LANE_TILES, LANES // SSM_GROUP
    eye = jnp.eye(gl, dtype=F32)

    def expand(bb):
        bb = bb.reshape(SSM_GROUP, nt, gl, SSM_STATE).transpose(1, 2, 0, 3)
        return jnp.einsum('tgip,gh->tgihp', bb, eye).reshape(nt, LANES, SSM_TILE_STATE)

    def contract(c):
        c = c.astype(F32).reshape(nt, gl, SSM_GROUP, SSM_STATE)
        return jnp.einsum('tgop,gh->tgpho', c, eye).reshape(nt, SSM_TILE_STATE, LANES)

    b_big = jnp.concatenate([expand(bbr), expand(bbi)], axis=-1).astype(BF16)
    c_big = jnp.concatenate([contract(c_re), -contract(c_im)], axis=1).astype(BF16)
    lam_r = jnp.broadcast_to(lbr, (batch, SSM_GROUPS * SSM_STATE))
    lam_i = jnp.broadcast_to(lbi, (batch, SSM_GROUPS * SSM_STATE))
    return b_big, c_big, lam_r, lam_i


def _s5_kernel(batch, u_ref, bbig_ref, cbig_ref, lr_ref, li_ref, dskip_ref, wglu_ref, y_ref,
               x_ref, st_ref, tb_ref):
    @pl.when(pl.program_id(0) == 0)
    def _():
        st_ref[...] = jnp.zeros_like(st_ref)

    ts = SSM_TILE_STATE
    tt = SSM_T_BLOCK
    blocks = range(SSM_BLOCKS_PER_STEP)
    tiles = range(SSM_LANE_TILES)
    for blk in blocks:
        for b in range(batch):
            for kt in tiles:
                c0 = b * SSM_WIDTH + kt * LANES
                tb_ref[blk, kt, pl.ds(b, tt, stride=batch), :] = (
                    u_ref[blk * tt:(blk + 1) * tt, c0:c0 + LANES].astype(F32))
    u_f32 = [[tb_ref[blk, kt] for kt in tiles] for blk in blocks]
    for blk in blocks:
        for kt in tiles:
            x_ref[blk, :, 2 * ts * kt:2 * ts * (kt + 1)] = jnp.dot(
                u_f32[blk][kt].astype(BF16), bbig_ref[kt], preferred_element_type=F32)

    for blk in blocks:
        for kt in tiles:
            re0, im0 = 2 * ts * kt, 2 * ts * kt + ts
            lr = lr_ref[:, kt * ts:(kt + 1) * ts]
            li = li_ref[:, kt * ts:(kt + 1) * ts]
            xr, xi = st_ref[:, re0:re0 + ts], st_ref[:, im0:im0 + ts]
            for t in range(tt):
                rows = slice(t * batch, (t + 1) * batch)
                xr, xi = (lr * xr - li * xi + x_ref[blk, rows, re0:re0 + ts],
                          lr * xi + li * xr + x_ref[blk, rows, im0:im0 + ts])
                x_ref[blk, rows, re0:re0 + ts] = xr
                x_ref[blk, rows, im0:im0 + ts] = xi
            st_ref[:, re0:re0 + ts] = xr
            st_ref[:, im0:im0 + ts] = xi

    for blk in blocks:
        ys = []
        for kt in tiles:
            xs = x_ref[blk, :, 2 * ts * kt:2 * ts * (kt + 1)].astype(BF16)
            ys.append(jnp.dot(xs, cbig_ref[kt], preferred_element_type=F32))
        y = jnp.concatenate(ys, axis=-1) + dskip_ref[...] * jnp.concatenate(u_f32[blk], axis=-1)
        z = _gelu_x2(y)
        y = (0.5 * z) * jax.nn.sigmoid(jnp.dot(z.astype(BF16), wglu_ref[...], preferred_element_type=F32))
        for kt in tiles:
            tb_ref[blk, kt] = y[:, kt * LANES:(kt + 1) * LANES]
        for b in range(batch):
            for kt in tiles:
                c0 = b * SSM_WIDTH + kt * LANES
                y_ref[blk * tt:(blk + 1) * tt, c0:c0 + LANES] = (
                    tb_ref[blk, kt, pl.ds(b, tt, stride=batch), :].astype(BF16))


def _s5(u_sb, mats, d_skip, w_glu, batch, seq):
    b_big, c_big, lam_r, lam_i = mats
    nblk = SSM_BLOCKS_PER_STEP
    rows = SSM_T_BLOCK * batch
    assert seq % (nblk * SSM_T_BLOCK) == 0 and batch % 8 == 0
    n_state = 2 * SSM_TILE_STATE * SSM_LANE_TILES
    blk = pl.BlockSpec((nblk * SSM_T_BLOCK, batch * SSM_WIDTH), lambda j: (j, 0))
    return pl.pallas_call(
        functools.partial(_s5_kernel, batch),
        grid=(seq // (nblk * SSM_T_BLOCK),),
        in_specs=[blk, _resident(b_big.shape), _resident(c_big.shape), _resident(lam_r.shape),
                  _resident(lam_i.shape), _resident((1, SSM_WIDTH)), _resident((SSM_WIDTH, SSM_WIDTH))],
        out_specs=blk,
        out_shape=jax.ShapeDtypeStruct((seq, batch * SSM_WIDTH), BF16),
        scratch_shapes=[pltpu.VMEM((nblk, rows, n_state), F32), pltpu.VMEM((batch, n_state), F32),
                        pltpu.VMEM((nblk, SSM_LANE_TILES, rows, LANES), F32)],
        compiler_params=pltpu.CompilerParams(
            dimension_semantics=("arbitrary",), vmem_limit_bytes=_vmem_limit(56 * 1024 * 1024)),
        name="s5_scan",
    )(u_sb, b_big, c_big, lam_r, lam_i, d_skip, w_glu)


MERGE_ROW_TILE = 2 * ROW_TILE


def _merge_kernel(att_ref, ret_ref, yc_ref, h_ref, wg_ref, wa_ref, wb_ref, wc_ref, wo_ref,
                  g_ref, b_ref, out_ref):
    d = D_MODEL
    for r0 in range(0, MERGE_ROW_TILE, ROW_TILE):
        rows = slice(r0, r0 + ROW_TILE)
        h = h_ref[rows, :]
        hb = h.astype(BF16)
        merged = None
        for i, (x_ref, w_ref) in enumerate(((att_ref, wa_ref), (ret_ref, wb_ref), (yc_ref, wc_ref))):
            gate = jax.nn.sigmoid(jnp.dot(hb, wg_ref[:, i * d:(i + 1) * d], preferred_element_type=F32))
            term = gate * jnp.dot(x_ref[rows, :], w_ref[...].astype(BF16), preferred_element_type=F32)
            merged = term if merged is None else merged + term
        y = DEEPNORM_ALPHA * h + jnp.dot(merged.astype(BF16), wo_ref[...].astype(BF16),
                                         preferred_element_type=F32)
        out_ref[rows, :] = _layer_norm(y, g_ref[...], b_ref[...])


def _merge(att_o, ret_o, yc_sb, h, w_gate, wa, wb, wc, wo, ln_g, ln_b, batch, seq):
    m = h.shape[0]
    tm = MERGE_ROW_TILE
    assert seq % tm == 0
    n_s = seq // tm
    row = lambda w: pl.BlockSpec((tm, w), lambda i: (i, 0))
    return pl.pallas_call(
        _merge_kernel,
        grid=(m // tm,),
        in_specs=[row(ATT_WIDTH), row(RET_V_WIDTH),
                  pl.BlockSpec((tm, SSM_WIDTH), lambda i: (i % n_s, i // n_s)),
                  row(D_MODEL), _resident(w_gate.shape),
                  _resident(wa.shape), _resident(wb.shape), _resident(wc.shape), _resident(wo.shape),
                  _resident((1, D_MODEL)), _resident((1, D_MODEL))],
        out_specs=row(D_MODEL),
        out_shape=jax.ShapeDtypeStruct((m, D_MODEL), F32),
        compiler_params=pltpu.CompilerParams(
            dimension_semantics=("arbitrary",), vmem_limit_bytes=_vmem_limit(54 * 1024 * 1024)),
        name="gated_merge",
    )(att_o, ret_o, yc_sb, h, w_gate, wa, wb, wc, wo, ln_g, ln_b)


FF_T_BLOCK = 128
FF_ROW_PARTS = 4


def _ffn_kernel(batch, h_ref, wup_ref, cw_ref, cb_ref, wdn_ref, g_ref, b_ref, out_ref,
                tb_ref, halo_ref, chunk_ref, act_ref):
    tt = FF_T_BLOCK
    rows = tt * batch
    halo = 2 * batch
    n_lane_tiles = D_MODEL // LANES

    @pl.when(pl.program_id(0) == 0)
    def _():
        halo_ref[...] = jnp.zeros_like(halo_ref)

    for b in range(batch):
        for c in range(n_lane_tiles):
            tb_ref[c, pl.ds(b, tt, stride=batch), :] = h_ref[b, :, c * LANES:(c + 1) * LANES]
    hb = jnp.concatenate([tb_ref[c] for c in range(n_lane_tiles)], axis=-1).astype(BF16)

    def conv(buf, col):
        cols = slice(col, col + FF_CHUNK)
        buf[0:halo, :] = halo_ref[:, cols]
        buf[halo:halo + rows, :] = jnp.dot(hb, wup_ref[:, cols], preferred_element_type=F32)
        halo_ref[:, cols] = buf[rows:rows + halo, :]
        w = cw_ref[:, cols]
        return (buf[0:rows, :] * w[0:1] + buf[batch:batch + rows, :] * w[1:2]
                + buf[halo:halo + rows, :] * w[2:3] + cb_ref[:, cols])

    for j in range(D_FF // FF_CHUNK):
        a = conv(chunk_ref.at[j % 2, 0], j * FF_CHUNK)
        g_half = conv(chunk_ref.at[j % 2, 1], D_FF + j * FF_CHUNK)
        act_ref[:, j * FF_CHUNK:(j + 1) * FF_CHUNK] = (_gelu_x2(a) * g_half).astype(BF16)

    tp = tt // FF_ROW_PARTS
    nr = tp * batch
    for part in range(FF_ROW_PARTS):
        r0 = part * nr
        h = jnp.concatenate([tb_ref[c, r0:r0 + nr, :] for c in range(n_lane_tiles)], axis=-1)
        y = DEEPNORM_ALPHA * h + jnp.dot(act_ref[r0:r0 + nr, :], wdn_ref[...], preferred_element_type=F32)
        y = _layer_norm(y, g_ref[...], b_ref[...])
        for c in range(n_lane_tiles):
            tb_ref[c, r0:r0 + nr, :] = y[:, c * LANES:(c + 1) * LANES]
        for b in range(batch):
            for c in range(n_lane_tiles):
                out_ref[b, part * tp:(part + 1) * tp, c * LANES:(c + 1) * LANES] = (
                    tb_ref[c, pl.ds(r0 + b, tp, stride=batch), :])


def _ffn(h, w_up, conv_w, conv_b, w_down, ln_g, ln_b, batch, seq):
    tt = FF_T_BLOCK
    rows = tt * batch
    assert D_FF % FF_CHUNK == 0 and tt % FF_ROW_PARTS == 0 and seq % tt == 0 and batch % 8 == 0
    half_gate = jnp.concatenate([jnp.ones((D_FF,), F32), jnp.full((D_FF,), 0.5, F32)])
    blk = pl.BlockSpec((batch, tt, D_MODEL), lambda j: (0, j, 0))
    out = pl.pallas_call(
        functools.partial(_ffn_kernel, batch),
        grid=(seq // tt,),
        in_specs=[blk, _resident(w_up.shape), _resident(conv_w.shape), _resident((1, 2 * D_FF)),
                  _resident(w_down.shape), _resident((1, D_MODEL)), _resident((1, D_MODEL))],
        out_specs=blk,
        out_shape=jax.ShapeDtypeStruct((batch, seq, D_MODEL), F32),
        scratch_shapes=[pltpu.VMEM((D_MODEL // LANES, rows, LANES), F32),
                        pltpu.VMEM((2 * batch, 2 * D_FF), F32),
                        pltpu.VMEM((2, 2, rows + 2 * batch, FF_CHUNK), F32),
                        pltpu.VMEM((rows, D_FF), BF16)],
        compiler_params=pltpu.CompilerParams(
            dimension_semantics=("arbitrary",), vmem_limit_bytes=_vmem_limit(58 * 1024 * 1024)),
        name="conv_ffn",
    )(h.reshape(batch, seq, D_MODEL), w_up, conv_w * half_gate, conv_b * half_gate, w_down, ln_g, ln_b)
    return out.reshape(batch * seq, D_MODEL)


def _rotary_tables(seq):
    half = RET_QK_DIM // 2
    inv = ROPE_BASE ** (-jnp.arange(0, RET_QK_DIM, 2, dtype=F32) / RET_QK_DIM)
    ang = jnp.arange(seq, dtype=F32)[:, None] * inv[None, :]
    cos, sin = jnp.cos(ang), jnp.sin(ang)
    assert cos.shape[1] == half
    heads_per_tile = LANES // RET_QK_DIM
    cos_t = jnp.tile(jnp.concatenate([cos, cos], -1), (1, heads_per_tile))
    sin_t = jnp.tile(jnp.concatenate([-sin, sin], -1), (1, heads_per_tile))
    return cos_t, sin_t


def kernel(x, ln_in_g, ln_in_b, w_in, rel_bias, w_proj_a, w_proj_b, w_proj_c, lam_re, lam_im, log_step,
           b_re, b_im, c_re, c_im, d_skip, w_glu, w_o, ln1_g, ln1_b, w_up, conv_w, conv_b, w_down,
           ln2_g, ln2_b):
    batch, seq, d = x.shape
    assert d == D_MODEL and w_in.shape == (DEPTH, D_MODEL, IN_WIDTH)
    m = batch * seq
    vec = lambda a: a.reshape(1, -1).astype(F32)
    cos_t, sin_t = _rotary_tables(seq)
    ret_tabs = _ret_tables()
    h = x.reshape(m, d)
    for l in range(DEPTH):
        w_mix = w_in[l, :, :MIX_WIDTH].astype(BF16)
        w_gate = w_in[l, :, MIX_WIDTH:].astype(BF16)
        if l == 0:
            h, att_qkv, ret_qk, ret_v, ret_g, u_sb = _in_proj(
                h, vec(ln_in_g), vec(ln_in_b), w_mix, cos_t, sin_t, batch, seq)
        else:
            att_qkv, ret_qk, ret_v, ret_g, u_sb = _in_proj(
                h, None, None, w_mix, cos_t, sin_t, batch, seq)
        att_o = _attention(att_qkv, _att_bias_generator(rel_bias[l]), batch, seq)
        ret_o = _retention(ret_qk, ret_v, ret_g, ret_tabs, batch, seq)
        lbr, lbi, bbr, bbi = _s5_discretise(lam_re[l], lam_im[l], log_step[l], b_re[l], b_im[l])
        mats = _s5_matrices(lbr, lbi, bbr, bbi, c_re[l], c_im[l], batch)
        yc_sb = _s5(u_sb, mats, vec(d_skip[l]), (0.5 * w_glu[l]).astype(BF16), batch, seq)
        h = _merge(att_o, ret_o, yc_sb, h, w_gate, w_proj_a[l].astype(F32), w_proj_b[l].astype(F32),
                   w_proj_c[l].astype(F32), w_o[l].astype(F32), vec(ln1_g[l]), vec(ln1_b[l]), batch, seq)
        h = _ffn(h, w_up[l].astype(BF16), conv_w[l].astype(F32), vec(conv_b[l]), w_down[l].astype(BF16),
                 vec(ln2_g[l]), vec(ln2_b[l]), batch, seq)
    return h.reshape(batch, seq, d)
```
